```python
import jax, jax.numpy as jnp
from jax import lax
import numpy as np

D_MODEL = 1024
BATCH = 32
SEQ = 256
DEPTH = 2
DEC_BATCH = 8
DEC_SEQ = 4096
PAST_LEN = 256

GRID_W = 64
N_MIXERS = 2
N_MLA_LAYERS = (DEPTH + 1) // 2
N_NA_LAYERS = DEPTH // 2
MLA_HEADS = 16
Q_LORA_RANK = 256
KV_LORA_RANK = 128
QK_NOPE_DIM = 128
QK_ROPE_DIM = 64
V_HEAD_DIM = 128
MLA_WIDTH = MLA_HEADS * V_HEAD_DIM
MLA_SCALE = (QK_NOPE_DIM + QK_ROPE_DIM) ** -0.5
ROPE_AXIS_FREQS = QK_ROPE_DIM // 4
ROPE_THETA = 10000.0
Q_BLOCK = 128
NA_HEADS = 16
NA_HEAD_DIM = 64
NA_WIDTH = NA_HEADS * NA_HEAD_DIM
NA_MAX_ROWS = 8
NA_COLS = 16
NA_SCALE = NA_HEAD_DIM ** -0.5
EPS = 1e-6

kernel_name = "hybrid_mla_natten_dit_step"


def rms_norm(x, g):
    xf = x.astype(jnp.float32)
    y = xf * lax.rsqrt(jnp.mean(xf * xf, axis=-1, keepdims=True) + EPS)
    return (y * g.astype(jnp.float32)).astype(x.dtype)


def modulation(cond, w, b):
    return jnp.split(jax.nn.silu(cond) @ w + b, 3, axis=-1)


def softmax_f32(s):
    return jax.nn.softmax(s.astype(jnp.float32), axis=-1)


def grid_rope_tables(n):
    t = jnp.arange(n)
    pos = jnp.stack([t // GRID_W, t % GRID_W], axis=-1).astype(jnp.float32)
    inv = ROPE_THETA ** (-jnp.arange(ROPE_AXIS_FREQS, dtype=jnp.float32) / ROPE_AXIS_FREQS)
    ang = pos[:, :, None] * inv
    return jnp.cos(ang), jnp.sin(ang)


def axial_rope(x, cos, sin):
    xs = x.reshape(x.shape[:-1] + (2, 2, ROPE_AXIS_FREQS))
    x1, x2 = xs[..., 0, :], xs[..., 1, :]
    out = jnp.stack([x1 * cos - x2 * sin, x2 * cos + x1 * sin], axis=-2)
    return out.reshape(x.shape).astype(x.dtype)


def mla_project(h, w_in, q_norm_g, w_qb, kv_norm_g):
    splits = [Q_LORA_RANK, Q_LORA_RANK + KV_LORA_RANK, Q_LORA_RANK + KV_LORA_RANK + QK_ROPE_DIM]
    q_a, kv_a, k_rope, gate = jnp.split(h @ w_in, splits, axis=-1)
    q = (rms_norm(q_a, q_norm_g) @ w_qb).reshape(h.shape[:2] + (MLA_HEADS, QK_NOPE_DIM + QK_ROPE_DIM))
    c_kv = rms_norm(kv_a, kv_norm_g)
    return q[..., :QK_NOPE_DIM], q[..., QK_NOPE_DIM:], c_kv, k_rope, gate


def mla_expand(c_kv, w_kvb):
    kv = (c_kv @ w_kvb).reshape(c_kv.shape[:2] + (MLA_HEADS, QK_NOPE_DIM + V_HEAD_DIM))
    return kv[..., :QK_NOPE_DIM], kv[..., QK_NOPE_DIM:]


def mla_attend(q_nope, q_rope, k_nope, k_rope, v):
    s = (jnp.einsum('bqhd,bkhd->bhqk', q_nope, k_nope)
         + jnp.einsum('bqhr,bkr->bhqk', q_rope, k_rope)) * MLA_SCALE
    p = softmax_f32(s).astype(v.dtype)
    return jnp.einsum('bhqk,bkhd->bqhd', p, v)


def mla_output(o, gate, w_out):
    b, n = o.shape[:2]
    return (o.reshape(b, n, MLA_WIDTH) * jax.nn.silu(gate)) @ w_out


def mla_context(h, w_in, q_norm_g, w_qb, kv_norm_g, w_kvb, w_out):
    q_nope, q_rope, c_kv, k_rope, gate = mla_project(h, w_in, q_norm_g, w_qb, kv_norm_g)
    k_nope, v = mla_expand(c_kv, w_kvb)
    o = mla_attend(q_nope, q_rope, k_nope, k_rope, v)
    return mla_output(o, gate, w_out), c_kv, k_rope


def mla_latent(h, ckv_ctx, krope_ctx, w_in, q_norm_g, w_qb, kv_norm_g, w_kvb, w_out):
    b, n, _ = h.shape
    q_nope, q_rope, c_kv, k_rope, gate = mla_project(h, w_in, q_norm_g, w_qb, kv_norm_g)
    cos, sin = grid_rope_tables(n)
    q_rope = axial_rope(q_rope, cos[:, None], sin[:, None])
    k_rope = axial_rope(k_rope, cos, sin)
    k_nope, v = mla_expand(jnp.concatenate([ckv_ctx, c_kv], axis=1), w_kvb)
    k_rope_all = jnp.concatenate([krope_ctx, k_rope], axis=1)
    nb = n // Q_BLOCK
    qn_b = q_nope.reshape(b, nb, Q_BLOCK, MLA_HEADS, QK_NOPE_DIM).transpose(1, 0, 2, 3, 4)
    qr_b = q_rope.reshape(b, nb, Q_BLOCK, MLA_HEADS, QK_ROPE_DIM).transpose(1, 0, 2, 3, 4)
    o = lax.map(lambda qs: mla_attend(qs[0], qs[1], k_nope, k_rope_all, v), (qn_b, qr_b))
    o = o.transpose(1, 0, 2, 3, 4).reshape(b, n, MLA_HEADS, V_HEAD_DIM)
    return mla_output(o, gate, w_out)


def na_project(h, w_in):
    q, k, v, gate = jnp.split(h @ w_in, 4, axis=-1)
    shp = h.shape[:2] + (NA_HEADS, NA_HEAD_DIM)
    return q.reshape(shp), k.reshape(shp), v.reshape(shp), gate


def na_context(h, w_in, w_out):
    b, n, _ = h.shape
    q, k, v, gate = na_project(h, w_in)
    p = softmax_f32(jnp.einsum('bqhd,bkhd->bhqk', q, k) * NA_SCALE).astype(v.dtype)
    o = jnp.einsum('bhqk,bkhd->bqhd', p, v).reshape(b, n, NA_WIDTH)
    return (o * jax.nn.silu(gate)) @ w_out, k, v


def na_latent(h, k_ctx, v_ctx, w_in, rel_bias, w_out):
    b, n, _ = h.shape
    rows = n // GRID_W
    kr = min(NA_MAX_ROWS, rows)
    n_loc = kr * GRID_W
    q, k, v, gate = na_project(h, w_in)
    grid_shape = (b, rows, GRID_W, NA_HEADS, NA_HEAD_DIM)
    qg, kg, vg = q.reshape(grid_shape), k.reshape(grid_shape), v.reshape(grid_shape)
    cols = jnp.arange(GRID_W)
    col_start = jnp.clip(cols - NA_COLS // 2, 0, GRID_W - NA_COLS)
    col_ok = (cols[None, :] >= col_start[:, None]) & (cols[None, :] < col_start[:, None] + NA_COLS)
    dc_idx = jnp.clip(cols[None, :] - cols[:, None] + NA_COLS - 1, 0, 2 * NA_COLS - 2)
    mask = jnp.broadcast_to(col_ok[:, None, :], (GRID_W, kr, GRID_W)).reshape(GRID_W, n_loc)

    def row_block(r):
        rs = jnp.clip(r - kr // 2, 0, rows - kr)
        q_r = lax.dynamic_index_in_dim(qg, r, axis=1, keepdims=False)
        k_blk = lax.dynamic_slice_in_dim(kg, rs, kr, axis=1).reshape(b, n_loc, NA_HEADS, NA_HEAD_DIM)
        v_blk = lax.dynamic_slice_in_dim(vg, rs, kr, axis=1).reshape(b, n_loc, NA_HEADS, NA_HEAD_DIM)
        dr_idx = rs + jnp.arange(kr) - r + NA_MAX_ROWS - 1
        bias = rel_bias[:, dr_idx[:, None, None], dc_idx[None, :, :]]
        bias = bias.transpose(0, 2, 1, 3).reshape(NA_HEADS, GRID_W, n_loc).astype(jnp.float32)
        s_loc = jnp.einsum('bqhd,bkhd->bhqk', q_r, k_blk).astype(jnp.float32) * NA_SCALE + bias
        s_loc = jnp.where(mask, s_loc, -jnp.inf)
        s_ctx = jnp.einsum('bqhd,bkhd->bhqk', q_r, k_ctx).astype(jnp.float32) * NA_SCALE
        p = softmax_f32(jnp.concatenate([s_loc, s_ctx], axis=-1)).astype(v.dtype)
        return (jnp.einsum('bhqk,bkhd->bqhd', p[..., :n_loc], v_blk)
                + jnp.einsum('bhqk,bkhd->bqhd', p[..., n_loc:], v_ctx))

    o = lax.map(row_block, jnp.arange(rows))
    o = o.transpose(1, 0, 2, 3, 4).reshape(b, n, NA_WIDTH)
    return (o * jax.nn.silu(gate)) @ w_out


def setup_inputs(seed: int = 0) -> dict:
    key = jax.random.key(seed)
    ks = jax.random.split(key, 24)
    nrm = jax.random.normal
    mla_in = Q_LORA_RANK + KV_LORA_RANK + QK_ROPE_DIM + MLA_WIDTH
    return {
        "x_prompt": nrm(ks[0], (BATCH, SEQ, D_MODEL), jnp.float32),
        "x_sample": nrm(ks[1], (DEC_BATCH, DEC_SEQ, D_MODEL), jnp.float32),
        "cache_mla_ckv": nrm(ks[2], (DEC_BATCH, N_MLA_LAYERS, PAST_LEN, KV_LORA_RANK), jnp.float32),
        "cache_mla_krope": nrm(ks[3], (DEC_BATCH, N_MLA_LAYERS, PAST_LEN, QK_ROPE_DIM), jnp.float32),
        "cache_na_k": nrm(ks[4], (DEC_BATCH, N_NA_LAYERS, PAST_LEN, NA_HEADS, NA_HEAD_DIM), jnp.float32),
        "cache_na_v": nrm(ks[5], (DEC_BATCH, N_NA_LAYERS, PAST_LEN, NA_HEADS, NA_HEAD_DIM), jnp.float32),
        "c": nrm(ks[6], (DEC_BATCH, D_MODEL), jnp.float32),
        "c_ctx": nrm(ks[7], (D_MODEL,), jnp.float32),
        "w_ada": nrm(ks[8], (DEPTH, D_MODEL, 3 * D_MODEL), jnp.float32) * D_MODEL ** -0.5,
        "b_ada": nrm(ks[9], (DEPTH, 3 * D_MODEL), jnp.float32) * 0.02,
        "pre_norm_g": 1.0 + 0.05 * nrm(ks[10], (DEPTH, D_MODEL), jnp.float32),
        "post_norm_g": 1.0 + 0.05 * nrm(ks[11], (DEPTH, D_MODEL), jnp.float32),
        "mla_w_in": nrm(ks[12], (N_MLA_LAYERS, D_MODEL, mla_in), jnp.float32) * D_MODEL ** -0.5,
        "mla_q_norm_g": 1.0 + 0.05 * nrm(ks[13], (N_MLA_LAYERS, Q_LORA_RANK), jnp.float32),
        "mla_w_qb": nrm(ks[14], (N_MLA_LAYERS, Q_LORA_RANK, MLA_HEADS * (QK_NOPE_DIM + QK_ROPE_DIM)), jnp.float32) * Q_LORA_RANK ** -0.5,
        "mla_kv_norm_g": 1.0 + 0.05 * nrm(ks[15], (N_MLA_LAYERS, KV_LORA_RANK), jnp.float32),
        "mla_w_kvb": nrm(ks[16], (N_MLA_LAYERS, KV_LORA_RANK, MLA_HEADS * (QK_NOPE_DIM + V_HEAD_DIM)), jnp.float32) * KV_LORA_RANK ** -0.5,
        "mla_w_out": nrm(ks[17], (N_MLA_LAYERS, MLA_WIDTH, D_MODEL), jnp.float32) * MLA_WIDTH ** -0.5,
        "na_w_in": nrm(ks[18], (N_NA_LAYERS, D_MODEL, 4 * NA_WIDTH), jnp.float32) * D_MODEL ** -0.5,
        "na_rel_bias": nrm(ks[19], (N_NA_LAYERS, NA_HEADS, 2 * NA_MAX_ROWS - 1, 2 * NA_COLS - 1), jnp.float32) * 0.5,
        "na_w_out": nrm(ks[20], (N_NA_LAYERS, NA_WIDTH, D_MODEL), jnp.float32) * NA_WIDTH ** -0.5,
    }


def reference(x_prompt, x_sample, cache_mla_ckv, cache_mla_krope, cache_na_k, cache_na_v, c, c_ctx,
              w_ada, b_ada, pre_norm_g, post_norm_g, mla_w_in, mla_q_norm_g, mla_w_qb, mla_kv_norm_g,
              mla_w_kvb, mla_w_out, na_w_in, na_rel_bias, na_w_out):
    xp, xs = x_prompt, x_sample
    new_ckv, new_krope, new_k, new_v = [], [], [], []
    for i in range(DEPTH):
        j = i // N_MIXERS
        sh_p, sc_p, g_p = modulation(c_ctx, w_ada[i], b_ada[i])
        sh_s, sc_s, g_s = modulation(c, w_ada[i], b_ada[i])
        hp = rms_norm(xp, pre_norm_g[i]) * (1.0 + sc_p) + sh_p
        hs = rms_norm(xs, pre_norm_g[i]) * (1.0 + sc_s[:, None]) + sh_s[:, None]
        if i % N_MIXERS == 0:
            op, ckv, krope = mla_context(hp, mla_w_in[j], mla_q_norm_g[j], mla_w_qb[j], mla_kv_norm_g[j],
                                         mla_w_kvb[j], mla_w_out[j])
            osm = mla_latent(hs, cache_mla_ckv[:, j], cache_mla_krope[:, j], mla_w_in[j], mla_q_norm_g[j],
                             mla_w_qb[j], mla_kv_norm_g[j], mla_w_kvb[j], mla_w_out[j])
            new_ckv.append(ckv)
            new_krope.append(krope)
        else:
            op, kc, vc = na_context(hp, na_w_in[j], na_w_out[j])
            osm = na_latent(hs, cache_na_k[:, j], cache_na_v[:, j], na_w_in[j], na_rel_bias[j], na_w_out[j])
            new_k.append(kc)
            new_v.append(vc)
        xp = xp + g_p * rms_norm(op, post_norm_g[i])
        xs = xs + g_s[:, None] * rms_norm(osm, post_norm_g[i])
    state_mla_ckv = jnp.stack(new_ckv, axis=1)
    state_mla_krope = jnp.stack(new_krope, axis=1)
    state_na_k = jnp.stack(new_k, axis=1)
    state_na_v = jnp.stack(new_v, axis=1)
    return (xp, xs, state_mla_ckv, state_mla_krope, state_na_k, state_na_v)
```

```python
import functools
import math

import numpy as np
import jax
import jax.numpy as jnp
from jax import lax
from jax.experimental import pallas as pl
from jax.experimental.pallas import tpu as pltpu

F32 = jnp.float32
BF16 = jnp.bfloat16

D_MODEL = 1024
GRID_W = 64
MLA_HEADS = 16
Q_LORA_RANK = 256
KV_LORA_RANK = 128
QK_NOPE_DIM = 128
QK_ROPE_DIM = 64
V_HEAD_DIM = 128
MLA_WIDTH = MLA_HEADS * V_HEAD_DIM
ROPE_AXIS_FREQS = QK_ROPE_DIM // 4
ROPE_THETA = 10000.0
NA_HEADS = 16
NA_HEAD_DIM = 64
NA_WIDTH = NA_HEADS * NA_HEAD_DIM
NA_MAX_ROWS = 8
NA_COLS = 16
EPS = 1e-6
LOG2E = math.log2(math.e)
MLA_QSCALE = (QK_NOPE_DIM + QK_ROPE_DIM) ** -0.5 * LOG2E
NA_QSCALE = NA_HEAD_DIM ** -0.5 * LOG2E

LANES = 128
MXU_DIM = 256
MOD_ROWS = 16
CTX_ROW = 8
VMEM_LIMIT = 48 * 1024 * 1024

NA_QROWS = 4
NA_WIN_ROWS = NA_QROWS + NA_MAX_ROWS

_NT = (((1,), (1,)), ((), ()))


def _params(sem, vmem=VMEM_LIMIT):
    return pltpu.CompilerParams(dimension_semantics=sem, vmem_limit_bytes=vmem)


def _rms(x, g):
    return x * lax.rsqrt(jnp.mean(x * x, axis=-1, keepdims=True) + EPS) * g


def _silu(x):
    return x * jax.nn.sigmoid(x)


def _mod_kernel(cond_ref, w_ref, b_ref, o_ref):
    o_ref[0] = jnp.dot(_silu(cond_ref[...]), w_ref[0], preferred_element_type=F32) + b_ref[0]


def _modulation(cond, w_ada, b_ada):
    depth, d, n = w_ada.shape
    bn = 768
    return pl.pallas_call(
        _mod_kernel,
        grid=(depth, n // bn),
        in_specs=[pl.BlockSpec((MOD_ROWS, d), lambda i, j: (0, 0)),
                  pl.BlockSpec((1, d, bn), lambda i, j: (i, 0, j)),
                  pl.BlockSpec((1, 1, bn), lambda i, j: (i, 0, j))],
        out_specs=pl.BlockSpec((1, MOD_ROWS, bn), lambda i, j: (i, 0, j)),
        out_shape=jax.ShapeDtypeStruct((depth, MOD_ROWS, n), F32),
        compiler_params=_params(("parallel", "parallel")),
        name="modulation",
    )(cond, w_ada, b_ada.reshape(depth, 1, n))


def _modulated(x, mod, g):
    return _rms(x, g) * (1.0 + mod[:, D_MODEL:2 * D_MODEL]) + mod[:, :D_MODEL]


def _mla_in_kernel(x_ref, mod_ref, preg_ref, win_ref, qg_ref, wqb_ref, kvg_ref, cos_ref, sin_ref,
                   q_ref, ckvb_ref, krb_ref, sg_ref, *state_refs):
    h = _modulated(x_ref[0], mod_ref[0], preg_ref[...]).astype(BF16)
    ya = jnp.dot(h, win_ref[:, :512], preferred_element_type=F32)
    ckv = _rms(ya[:, 256:384], kvg_ref[...])
    kr2 = ya[:, 384:512]
    cos2 = cos_ref[...]
    sin2 = sin_ref[...]
    ckvb_ref[0] = ckv.astype(BF16)
    krb_ref[0] = (kr2 * cos2 + pltpu.roll(kr2, 64, 1) * sin2).astype(BF16)
    if state_refs:
        state_refs[0][0] = ckv
        state_refs[1][0] = kr2[:, :QK_ROPE_DIM]
    qn = _rms(ya[:, :256], qg_ref[...]).astype(BF16)
    for hh in range(MLA_HEADS):
        yq = jnp.dot(qn, wqb_ref[:, hh * 256:(hh + 1) * 256], preferred_element_type=F32)
        sec = yq[:, 128:]
        q_ref[0, hh, :, 0:128] = (yq[:, :128] * MLA_QSCALE).astype(BF16)
        q_ref[0, hh, :, 128:256] = ((sec * cos2 + pltpu.roll(sec, 64, 1) * sin2) * MLA_QSCALE).astype(BF16)
    for c in range(MLA_WIDTH // 512):
        g = jnp.dot(h, win_ref[:, 512 + c * 512:1024 + c * 512], preferred_element_type=F32)
        sg_ref[0, :, c * 512:(c + 1) * 512] = _silu(g).astype(BF16)


def _mla_in(x, mod, mod_row, preg, win, qg, wqb, kvg, cos2, sin2, with_state, tm=512):
    b, t, d = x.shape
    nt = t // tm
    rope_blocks = cos2.shape[0] // tm
    tbl_idx = (lambda i, j: (j, 0)) if rope_blocks > 1 else (lambda i, j: (0, 0))
    mod_idx = (lambda i, j: (i, 0, 0)) if mod_row is None else (lambda i, j: (mod_row, 0, 0))
    const = lambda i, j: (0, 0)
    out_shape = [jax.ShapeDtypeStruct((b, MLA_HEADS, t, 256), BF16),
                 jax.ShapeDtypeStruct((b, t, 128), BF16),
                 jax.ShapeDtypeStruct((b, t, 128), BF16),
                 jax.ShapeDtypeStruct((b, t, MLA_WIDTH), BF16)]
    out_specs = [pl.BlockSpec((1, MLA_HEADS, tm, 256), lambda i, j: (i, 0, j, 0)),
                 pl.BlockSpec((1, tm, 128), lambda i, j: (i, j, 0)),
                 pl.BlockSpec((1, tm, 128), lambda i, j: (i, j, 0)),
                 pl.BlockSpec((1, tm, MLA_WIDTH), lambda i, j: (i, j, 0))]
    if with_state:
        out_shape += [jax.ShapeDtypeStruct((b, t, KV_LORA_RANK), F32),
                      jax.ShapeDtypeStruct((b, t, QK_ROPE_DIM), F32)]
        out_specs += [pl.BlockSpec((1, tm, KV_LORA_RANK), lambda i, j: (i, j, 0)),
                      pl.BlockSpec((1, tm, QK_ROPE_DIM), lambda i, j: (i, j, 0))]
    return pl.pallas_call(
        _mla_in_kernel,
        grid=(b, nt),
        in_specs=[pl.BlockSpec((1, tm, d), lambda i, j: (i, j, 0)),
                  pl.BlockSpec((1, 1, 3 * d), mod_idx),
                  pl.BlockSpec((1, d), const),
                  pl.BlockSpec(win.shape, const),
                  pl.BlockSpec((1, Q_LORA_RANK), const),
                  pl.BlockSpec(wqb.shape, const),
                  pl.BlockSpec((1, KV_LORA_RANK), const),
                  pl.BlockSpec((tm, 128), tbl_idx),
                  pl.BlockSpec((tm, 128), tbl_idx)],
        out_specs=out_specs,
        out_shape=out_shape,
        compiler_params=_params(("parallel", "parallel")),
        name="mla_in_proj",
    )(x, mod, preg, win, qg, wqb, kvg, cos2, sin2)


def _mla_attn_kernel(q_ref, ckv_ref, kr_ref, wkb_ref, wvbt_ref, o_ref, k_scr, vt_scr, *, nk, tk):
    @pl.when(pl.program_id(2) == 0)
    def _():
        def expand(i, carry):
            r = pl.multiple_of(i * MXU_DIM, MXU_DIM)
            c = ckv_ref[0, pl.ds(r, MXU_DIM), :]
            k_scr[pl.ds(r, MXU_DIM), 0:128] = jnp.dot(c, wkb_ref[0], preferred_element_type=F32).astype(BF16)
            k_scr[pl.ds(r, MXU_DIM), 128:256] = kr_ref[0, pl.ds(r, MXU_DIM), :]
            vt_scr[:, pl.ds(r, MXU_DIM)] = lax.dot_general(
                wvbt_ref[0], c, _NT, preferred_element_type=F32).astype(BF16)
            return carry
        lax.fori_loop(0, nk // MXU_DIM, expand, 0)

    q = q_ref[0, 0]
    tq = q.shape[0]

    def step(i, carry):
        m, l, acc = carry
        r = pl.multiple_of(i * tk, tk)
        st = lax.dot_general(k_scr[pl.ds(r, tk), :], q, _NT, preferred_element_type=F32)
        m_new = jnp.maximum(m, jnp.max(st, axis=0, keepdims=True))
        alpha = jnp.exp2(m - m_new)
        p = jnp.exp2(st - m_new)
        l = alpha * l + jnp.sum(p, axis=0, keepdims=True)
        acc = alpha * acc + jnp.dot(vt_scr[:, pl.ds(r, tk)], p.astype(BF16), preferred_element_type=F32)
        return m_new, l, acc

    m0 = jnp.full((1, tq), -jnp.inf, F32)
    l0 = jnp.zeros((1, tq), F32)
    acc0 = jnp.zeros((V_HEAD_DIM, tq), F32)
    _, l, acc = lax.fori_loop(0, nk // tk, step, (m0, l0, acc0))
    o_ref[0] = (acc / l).T.astype(BF16)


def _mla_attn(q, ckv_all, kr_all, wkb, wvbt, tq=256, tk=256):
    b, nh, t, _ = q.shape
    nk = ckv_all.shape[1]
    return pl.pallas_call(
        functools.partial(_mla_attn_kernel, nk=nk, tk=tk),
        grid=(b, nh, t // tq),
        in_specs=[pl.BlockSpec((1, 1, tq, 256), lambda i, h, j: (i, h, j, 0)),
                  pl.BlockSpec((1, nk, 128), lambda i, h, j: (i, 0, 0)),
                  pl.BlockSpec((1, nk, 128), lambda i, h, j: (i, 0, 0)),
                  pl.BlockSpec((1, 128, 128), lambda i, h, j: (h, 0, 0)),
                  pl.BlockSpec((1, 128, 128), lambda i, h, j: (h, 0, 0))],
        out_specs=pl.BlockSpec((1, tq, V_HEAD_DIM), lambda i, h, j: (i, j, h)),
        out_shape=jax.ShapeDtypeStruct((b, t, MLA_WIDTH), BF16),
        scratch_shapes=[pltpu.VMEM((nk, 256), BF16), pltpu.VMEM((V_HEAD_DIM, nk), BF16)],
        compiler_params=_params(("parallel", "parallel", "arbitrary")),
        name="mla_attention",
    )(q, ckv_all, kr_all, wkb, wvbt)


def _out_kernel(o_ref, sg_ref, x_ref, mod_ref, wout_ref, postg_ref, y_ref):
    og = (o_ref[0].astype(F32) * sg_ref[0].astype(F32)).astype(BF16)
    out = jnp.dot(og, wout_ref[...], preferred_element_type=F32)
    gate = mod_ref[0][:, 2 * D_MODEL:]
    y_ref[0] = x_ref[0] + gate * _rms(out, postg_ref[...])


def _out_proj(o, sg, x, mod, mod_row, wout, postg, tm=512):
    b, t, d = x.shape
    w = o.shape[-1]
    mod_idx = (lambda i, j: (i, 0, 0)) if mod_row is None else (lambda i, j: (mod_row, 0, 0))
    const = lambda i, j: (0, 0)
    return pl.pallas_call(
        _out_kernel,
        grid=(b, t // tm),
        in_specs=[pl.BlockSpec((1, tm, w), lambda i, j: (i, j, 0)),
                  pl.BlockSpec((1, tm, w), lambda i, j: (i, j, 0)),
                  pl.BlockSpec((1, tm, d), lambda i, j: (i, j, 0)),
                  pl.BlockSpec((1, 1, 3 * d), mod_idx),
                  pl.BlockSpec(wout.shape, const),
                  pl.BlockSpec((1, d), const)],
        out_specs=pl.BlockSpec((1, tm, d), lambda i, j: (i, j, 0)),
        out_shape=jax.ShapeDtypeStruct((b, t, d), F32),
        compiler_params=_params(("parallel", "parallel")),
        name="out_proj",
    )(o, sg, x, mod, wout, postg)


def _na_in_kernel(x_ref, mod_ref, preg_ref, win_ref, q_ref, k_ref, vt_ref, sg_ref, *state_refs):
    h = _modulated(x_ref[0], mod_ref[0], preg_ref[...]).astype(BF16)
    w = NA_WIDTH
    q = jnp.dot(h, win_ref[:, 0:w], preferred_element_type=F32)
    q_ref[0] = (q * NA_QSCALE).astype(BF16)
    k = jnp.dot(h, win_ref[:, w:2 * w], preferred_element_type=F32)
    k_ref[0] = k.astype(BF16)
    v = jnp.dot(h, win_ref[:, 2 * w:3 * w], preferred_element_type=F32)
    vt_ref[0] = v.T.astype(BF16)
    if state_refs:
        state_refs[0][0] = k
        state_refs[1][0] = v
    g = jnp.dot(h, win_ref[:, 3 * w:4 * w], preferred_element_type=F32)
    sg_ref[0] = _silu(g).astype(BF16)


def _na_in(x, mod, mod_row, preg, win, with_state, tm):
    b, t, d = x.shape
    w = NA_WIDTH
    mod_idx = (lambda i, j: (i, 0, 0)) if mod_row is None else (lambda i, j: (mod_row, 0, 0))
    const = lambda i, j: (0, 0)
    row_blk = pl.BlockSpec((1, tm, w), lambda i, j: (i, j, 0))
    out_shape = [jax.ShapeDtypeStruct((b, t, w), BF16), jax.ShapeDtypeStruct((b, t, w), BF16),
                 jax.ShapeDtypeStruct((b, w, t), BF16), jax.ShapeDtypeStruct((b, t, w), BF16)]
    out_specs = [row_blk, row_blk, pl.BlockSpec((1, w, tm), lambda i, j: (i, 0, j)), row_blk]
    if with_state:
        out_shape += [jax.ShapeDtypeStruct((b, t, w), F32), jax.ShapeDtypeStruct((b, t, w), F32)]
        out_specs += [row_blk, row_blk]
    return pl.pallas_call(
        _na_in_kernel,
        grid=(b, t // tm),
        in_specs=[pl.BlockSpec((1, tm, d), lambda i, j: (i, j, 0)),
                  pl.BlockSpec((1, 1, 3 * d), mod_idx),
                  pl.BlockSpec((1, d), const),
                  pl.BlockSpec(win.shape, const)],
        out_specs=out_specs,
        out_shape=out_shape,
        compiler_params=_params(("parallel", "parallel")),
        name="na_in_proj",
    )(x, mod, preg, win)


def _head_masks(shape):
    lane = lax.broadcasted_iota(jnp.int32, shape, 1)
    return (lane < NA_HEAD_DIM, lane >= NA_HEAD_DIM)


def _na_ctx_kernel(q_ref, k_ref, vt_ref, o_ref):
    q2 = q_ref[0]
    k2 = k_ref[0]
    outs = []
    for hl, msk in enumerate(_head_masks(q2.shape)):
        qm = jnp.where(msk, q2, jnp.zeros_like(q2))
        st = lax.dot_general(k2, qm, _NT, preferred_element_type=F32)
        p = jnp.exp2(st - jnp.max(st, axis=0, keepdims=True))
        l = jnp.sum(p, axis=0, keepdims=True)
        vt = vt_ref[0, hl * NA_HEAD_DIM:(hl + 1) * NA_HEAD_DIM, :]
        outs.append(jnp.dot(vt, p.astype(BF16), preferred_element_type=F32) / l)
    o_ref[0] = jnp.concatenate(outs, axis=0).T.astype(BF16)


def _na_ctx(q, k, vt):
    b, t, w = q.shape
    return pl.pallas_call(
        _na_ctx_kernel,
        grid=(b, w // LANES),
        in_specs=[pl.BlockSpec((1, t, LANES), lambda i, h: (i, 0, h)),
                  pl.BlockSpec((1, t, LANES), lambda i, h: (i, 0, h)),
                  pl.BlockSpec((1, LANES, t), lambda i, h: (i, h, 0))],
        out_specs=pl.BlockSpec((1, t, LANES), lambda i, h: (i, 0, h)),
        out_shape=jax.ShapeDtypeStruct((b, t, w), BF16),
        compiler_params=_params(("parallel", "parallel")),
        name="na_ctx_attention",
    )(q, k, vt)


def _na_lat_kernel(q_ref, k_ref, vt_ref, kc_ref, vct_ref, bias_ref, o_ref, *, nblk):
    blk = pl.program_id(2)
    kind = jnp.where(blk == 0, 0, jnp.where(blk == nblk - 1, 2, 1))
    ws = jnp.clip(blk * NA_QROWS - NA_MAX_ROWS // 2, 0, nblk * NA_QROWS - NA_WIN_ROWS)
    nwin = NA_WIN_ROWS * GRID_W
    start = pl.multiple_of(ws * GRID_W, MXU_DIM)
    q2 = q_ref[0]
    kwin = k_ref[0, pl.ds(start, nwin), :]
    kc = kc_ref[0]
    outs = []
    for hl, msk in enumerate(_head_masks(q2.shape)):
        qm = jnp.where(msk, q2, jnp.zeros_like(q2))
        s_loc = lax.dot_general(kwin, qm, _NT, preferred_element_type=F32) + bias_ref[hl, kind]
        s_ctx = lax.dot_general(kc, qm, _NT, preferred_element_type=F32)
        m = jnp.maximum(jnp.max(s_loc, axis=0, keepdims=True), jnp.max(s_ctx, axis=0, keepdims=True))
        p_loc = jnp.exp2(s_loc - m)
        p_ctx = jnp.exp2(s_ctx - m)
        l = jnp.sum(p_loc, axis=0, keepdims=True) + jnp.sum(p_ctx, axis=0, keepdims=True)
        rows = slice(hl * NA_HEAD_DIM, (hl + 1) * NA_HEAD_DIM)
        ot = (jnp.dot(vt_ref[0, rows, pl.ds(start, nwin)], p_loc.astype(BF16), preferred_element_type=F32)
              + jnp.dot(vct_ref[0, rows, :], p_ctx.astype(BF16), preferred_element_type=F32))
        outs.append(ot / l)
    o_ref[0] = jnp.concatenate(outs, axis=0).T.astype(BF16)


def _na_lat(q, k, vt, kc, vct, bias):
    b, t, w = q.shape
    tq = NA_QROWS * GRID_W
    nblk = t // tq
    nctx = kc.shape[1]
    nwin = NA_WIN_ROWS * GRID_W
    return pl.pallas_call(
        functools.partial(_na_lat_kernel, nblk=nblk),
        grid=(w // LANES, b, nblk),
        in_specs=[pl.BlockSpec((1, tq, LANES), lambda h, i, j: (i, j, h)),
                  pl.BlockSpec((1, t, LANES), lambda h, i, j: (i, 0, h)),
                  pl.BlockSpec((1, LANES, t), lambda h, i, j: (i, h, 0)),
                  pl.BlockSpec((1, nctx, LANES), lambda h, i, j: (i, 0, h)),
                  pl.BlockSpec((1, LANES, nctx), lambda h, i, j: (i, h, 0)),
                  pl.BlockSpec((2, 3, nwin, tq), lambda h, i, j: (h, 0, 0, 0))],
        out_specs=pl.BlockSpec((1, tq, LANES), lambda h, i, j: (i, j, h)),
        out_shape=jax.ShapeDtypeStruct((b, t, w), BF16),
        compiler_params=_params(("parallel", "parallel", "arbitrary")),
        name="na_lat_attention",
    )(q, k, vt, kc, vct, bias)


def _na_bias_tables(rel_bias, rows):
    nblk = rows // NA_QROWS
    cols = np.arange(GRID_W)
    col_start = np.clip(cols - NA_COLS // 2, 0, GRID_W - NA_COLS)
    col_ok = (cols[:, None] >= col_start[None, :]) & (cols[:, None] < col_start[None, :] + NA_COLS)
    dc = np.clip(cols[:, None] - cols[None, :] + NA_COLS - 1, 0, 2 * NA_COLS - 2)
    dr = np.zeros((3, NA_WIN_ROWS, NA_QROWS), np.int32)
    ok = np.zeros((3, NA_WIN_ROWS, NA_QROWS), bool)
    for kind, blk in enumerate((0, 1, nblk - 1)):
        r0 = blk * NA_QROWS
        ws = int(np.clip(r0 - NA_MAX_ROWS // 2, 0, rows - NA_WIN_ROWS))
        for j in range(NA_WIN_ROWS):
            for i in range(NA_QROWS):
                rs = int(np.clip(r0 + i - NA_MAX_ROWS // 2, 0, rows - NA_MAX_ROWS))
                ok[kind, j, i] = rs <= ws + j < rs + NA_MAX_ROWS
                dr[kind, j, i] = np.clip(ws + j - (r0 + i) + NA_MAX_ROWS - 1, 0, 2 * NA_MAX_ROWS - 2)
    tbl = rel_bias[:, dr[:, :, None, :, None], dc[None, None, :, None, :]]
    valid = ok[:, :, None, :, None] & col_ok[None, None, :, None, :]
    tbl = jnp.where(valid[None], tbl.astype(F32) * LOG2E, -jnp.inf)
    return tbl.reshape(rel_bias.shape[0], 3, NA_WIN_ROWS * GRID_W, NA_QROWS * GRID_W)


def _rotate_half_cols(w):
    w4 = w.reshape(w.shape[:-1] + (2, 2, ROPE_AXIS_FREQS))
    return jnp.stack([-w4[..., 1, :], w4[..., 0, :]], axis=-2).reshape(w.shape)


def _rope_tables(n):
    t = jnp.arange(n)
    pos = jnp.stack([t // GRID_W, t % GRID_W], axis=-1).astype(F32)
    inv = ROPE_THETA ** (-jnp.arange(ROPE_AXIS_FREQS, dtype=F32) / ROPE_AXIS_FREQS)
    ang = pos[:, :, None] * inv
    cos, sin = jnp.cos(ang), jnp.sin(ang)
    zeros = jnp.zeros((n, LANES - QK_ROPE_DIM), F32)
    cos2 = jnp.concatenate([cos[:, 0], cos[:, 0], cos[:, 1], cos[:, 1], zeros], axis=-1)
    sin2 = jnp.concatenate([sin[:, 0], sin[:, 0], sin[:, 1], sin[:, 1], zeros], axis=-1)
    return cos2.astype(F32), sin2.astype(F32)


def kernel(x_prompt, x_sample, cache_mla_ckv, cache_mla_krope, cache_na_k, cache_na_v, c, c_ctx,
           w_ada, b_ada, pre_norm_g, post_norm_g, mla_w_in, mla_q_norm_g, mla_w_qb, mla_kv_norm_g,
           mla_w_kvb, mla_w_out, na_w_in, na_rel_bias, na_w_out):
    bp, tp, d = x_prompt.shape
    bs, ts, _ = x_sample.shape
    tm = 512
    tmp = min(tm, tp)

    cond = jnp.zeros((MOD_ROWS, d), F32).at[:bs].set(c).at[CTX_ROW].set(c_ctx)
    mod = _modulation(cond, w_ada, b_ada)
    mod = mod.reshape(mod.shape[0], MOD_ROWS, 1, 3 * d)

    w_in = mla_w_in[0]
    n_small = Q_LORA_RANK + KV_LORA_RANK + QK_ROPE_DIM
    w_in_ext = jnp.concatenate(
        [w_in[:, :n_small], _rotate_half_cols(w_in[:, n_small - QK_ROPE_DIM:n_small]), w_in[:, n_small:]],
        axis=1).astype(BF16)
    wqb3 = mla_w_qb[0].reshape(Q_LORA_RANK, MLA_HEADS, QK_NOPE_DIM + QK_ROPE_DIM)
    wqb_ext = jnp.concatenate([wqb3, _rotate_half_cols(wqb3[..., QK_NOPE_DIM:])], axis=-1)
    wqb_ext = wqb_ext.reshape(Q_LORA_RANK, MLA_HEADS * 256).astype(BF16)
    wkvb3 = mla_w_kvb[0].reshape(KV_LORA_RANK, MLA_HEADS, QK_NOPE_DIM + V_HEAD_DIM)
    wkb = wkvb3[..., :QK_NOPE_DIM].transpose(1, 0, 2).astype(BF16)
    wvbt = wkvb3[..., QK_NOPE_DIM:].transpose(1, 2, 0).astype(BF16)
    preg0 = pre_norm_g[0].reshape(1, d)
    postg0 = post_norm_g[0].reshape(1, d)
    qg = mla_q_norm_g[0].reshape(1, Q_LORA_RANK)
    kvg = mla_kv_norm_g[0].reshape(1, KV_LORA_RANK)
    cos_s, sin_s = _rope_tables(ts)
    pad = jnp.zeros((tmp, LANES - QK_ROPE_DIM), F32)
    cos_p = jnp.concatenate([jnp.ones((tmp, QK_ROPE_DIM), F32), pad], axis=-1)
    sin_p = jnp.zeros((tmp, LANES), F32)
    w_out0 = mla_w_out[0].astype(BF16)

    qp, ckvb_p, krb_p, sg_p, ckv_p, kr_p = _mla_in(
        x_prompt, mod[0], CTX_ROW, preg0, w_in_ext, qg, wqb_ext, kvg, cos_p, sin_p, True, tmp)
    qs, ckvb_s, krb_s, sg_s = _mla_in(
        x_sample, mod[0], None, preg0, w_in_ext, qg, wqb_ext, kvg, cos_s, sin_s, False, tm)

    op = _mla_attn(qp, ckvb_p, krb_p, wkb, wvbt)
    ckv_all = jnp.concatenate([cache_mla_ckv[:, 0].astype(BF16), ckvb_s], axis=1)
    kr_ctx = jnp.pad(cache_mla_krope[:, 0], ((0, 0), (0, 0), (0, LANES - QK_ROPE_DIM))).astype(BF16)
    kr_all = jnp.concatenate([kr_ctx, krb_s], axis=1)
    osm = _mla_attn(qs, ckv_all, kr_all, wkb, wvbt)

    xp = _out_proj(op, sg_p, x_prompt, mod[0], CTX_ROW, w_out0, postg0, tmp)
    xs = _out_proj(osm, sg_s, x_sample, mod[0], None, w_out0, postg0, tm)

    preg1 = pre_norm_g[1].reshape(1, d)
    postg1 = post_norm_g[1].reshape(1, d)
    na_win = na_w_in[0].astype(BF16)
    w_out1 = na_w_out[0].astype(BF16)
    q1p, k1p, vt1p, sg1p, k_state, v_state = _na_in(xp, mod[1], CTX_ROW, preg1, na_win, True, tmp)
    q1s, k1s, vt1s, sg1s = _na_in(xs, mod[1], None, preg1, na_win, False, tm)
    o1p = _na_ctx(q1p, k1p, vt1p)
    npast = cache_na_k.shape[2]
    kc = cache_na_k[:, 0].reshape(bs, npast, NA_WIDTH).astype(BF16)
    vct = cache_na_v[:, 0].reshape(bs, npast, NA_WIDTH).transpose(0, 2, 1).astype(BF16)
    bias = _na_bias_tables(na_rel_bias[0], ts // GRID_W)
    o1s = _na_lat(q1s, k1s, vt1s, kc, vct, bias)
    yp = _out_proj(o1p, sg1p, xp, mod[1], CTX_ROW, w_out1, postg1, tmp)
    ys = _out_proj(o1s, sg1s, xs, mod[1], None, w_out1, postg1, tm)

    return (yp, ys,
            ckv_p.reshape(bp, 1, tp, KV_LORA_RANK),
            kr_p.reshape(bp, 1, tp, QK_ROPE_DIM),
            k_state.reshape(bp, 1, tp, NA_HEADS, NA_HEAD_DIM),
            v_state.reshape(bp, 1, tp, NA_HEADS, NA_HEAD_DIM))
```

```python
import functools
import math

import numpy as np
import jax
import jax.numpy as jnp
from jax import lax
from jax.experimental import pallas as pl
from jax.experimental.pallas import tpu as pltpu

F32 = jnp.float32
BF16 = jnp.bfloat16

D_MODEL = 1024
GRID_W = 64
MLA_HEADS = 16
Q_LORA_RANK = 256
KV_LORA_RANK = 128
QK_NOPE_DIM = 128
QK_ROPE_DIM = 64
V_HEAD_DIM = 128
MLA_WIDTH = MLA_HEADS * V_HEAD_DIM
ROPE_AXIS_FREQS = QK_ROPE_DIM // 4
ROPE_THETA = 10000.0
NA_HEADS = 16
NA_HEAD_DIM = 64
NA_WIDTH = NA_HEADS * NA_HEAD_DIM
NA_MAX_ROWS = 8
NA_COLS = 16
EPS = 1e-6
LOG2E = math.log2(math.e)
MLA_QSCALE = (QK_NOPE_DIM + QK_ROPE_DIM) ** -0.5 * LOG2E
NA_QSCALE = NA_HEAD_DIM ** -0.5 * LOG2E

LANES = 128
SUBLANES = 8
MXU_DIM = 256
MOD_ROWS = 16
CTX_ROW = 8
VMEM_LIMIT = 48 * 1024 * 1024

NA_QROWS = 4
NA_WIN_ROWS = NA_QROWS + NA_MAX_ROWS

_NT = (((1,), (1,)), ((), ()))


def _params(sem, vmem=VMEM_LIMIT):
    return pltpu.CompilerParams(dimension_semantics=sem, vmem_limit_bytes=vmem)


def _rms(x, g):
    return x * lax.rsqrt(jnp.mean(x * x, axis=-1, keepdims=True) + EPS) * g


def _silu(x):
    return x * jax.nn.sigmoid(x)


def _mod_kernel(cond_ref, w_ref, b_ref, o_ref):
    o_ref[0] = jnp.dot(_silu(cond_ref[...]), w_ref[0], preferred_element_type=F32) + b_ref[0]


def _modulation(cond, w_ada, b_ada):
    depth, d, n = w_ada.shape
    bn = 768
    return pl.pallas_call(
        _mod_kernel,
        grid=(depth, n // bn),
        in_specs=[pl.BlockSpec((MOD_ROWS, d), lambda i, j: (0, 0)),
                  pl.BlockSpec((1, d, bn), lambda i, j: (i, 0, j)),
                  pl.BlockSpec((1, 1, bn), lambda i, j: (i, 0, j))],
        out_specs=pl.BlockSpec((1, MOD_ROWS, bn), lambda i, j: (i, 0, j)),
        out_shape=jax.ShapeDtypeStruct((depth, MOD_ROWS, n), F32),
        compiler_params=_params(("parallel", "parallel")),
        name="modulation",
    )(cond, w_ada, b_ada.reshape(depth, 1, n))


def _modulated(x, mod, g):
    return _rms(x, g) * (1.0 + mod[:, D_MODEL:2 * D_MODEL]) + mod[:, :D_MODEL]


def _mla_in_kernel(x_ref, mod_ref, preg_ref, win_ref, qg_ref, wqb_ref, kvg_ref, cos_ref, sin_ref,
                   q_ref, ckvb_ref, krb_ref, sg_ref, *state_refs):
    h = _modulated(x_ref[0], mod_ref[0], preg_ref[...]).astype(BF16)
    ya = jnp.dot(h, win_ref[:, :512], preferred_element_type=F32)
    ckv = _rms(ya[:, 256:384], kvg_ref[...])
    kr2 = ya[:, 384:512]
    cos2 = cos_ref[...]
    sin2 = sin_ref[...]
    ckvb_ref[0] = ckv.astype(BF16)
    krb_ref[0] = (kr2 * cos2 + pltpu.roll(kr2, 64, 1) * sin2).astype(BF16)
    if state_refs:
        state_refs[0][0] = ckv
        state_refs[1][0] = kr2[:, :QK_ROPE_DIM]
    qn = _rms(ya[:, :256], qg_ref[...]).astype(BF16)
    for hh in range(MLA_HEADS):
        yq = jnp.dot(qn, wqb_ref[:, hh * 256:(hh + 1) * 256], preferred_element_type=F32)
        sec = yq[:, 128:]
        q_ref[0, hh, :, 0:128] = (yq[:, :128] * MLA_QSCALE).astype(BF16)
        q_ref[0, hh, :, 128:256] = ((sec * cos2 + pltpu.roll(sec, 64, 1) * sin2) * MLA_QSCALE).astype(BF16)
    for c in range(MLA_WIDTH // 512):
        g = jnp.dot(h, win_ref[:, 512 + c * 512:1024 + c * 512], preferred_element_type=F32)
        sg_ref[0, :, c * 512:(c + 1) * 512] = _silu(g).astype(BF16)


def _mla_in(x, mod, mod_row, preg, win, qg, wqb, kvg, cos2, sin2, with_state, tm=512):
    b, t, d = x.shape
    nt = t // tm
    rope_blocks = cos2.shape[0] // tm
    tbl_idx = (lambda i, j: (j, 0)) if rope_blocks > 1 else (lambda i, j: (0, 0))
    mod_idx = (lambda i, j: (i, 0, 0)) if mod_row is None else (lambda i, j: (mod_row, 0, 0))
    const = lambda i, j: (0, 0)
    out_shape = [jax.ShapeDtypeStruct((b, MLA_HEADS, t, 256), BF16),
                 jax.ShapeDtypeStruct((b, t, 128), BF16),
                 jax.ShapeDtypeStruct((b, t, 128), BF16),
                 jax.ShapeDtypeStruct((b, t, MLA_WIDTH), BF16)]
    out_specs = [pl.BlockSpec((1, MLA_HEADS, tm, 256), lambda i, j: (i, 0, j, 0)),
                 pl.BlockSpec((1, tm, 128), lambda i, j: (i, j, 0)),
                 pl.BlockSpec((1, tm, 128), lambda i, j: (i, j, 0)),
                 pl.BlockSpec((1, tm, MLA_WIDTH), lambda i, j: (i, j, 0))]
    if with_state:
        out_shape += [jax.ShapeDtypeStruct((b, t, KV_LORA_RANK), F32),
                      jax.ShapeDtypeStruct((b, t, QK_ROPE_DIM), F32)]
        out_specs += [pl.BlockSpec((1, tm, KV_LORA_RANK), lambda i, j: (i, j, 0)),
                      pl.BlockSpec((1, tm, QK_ROPE_DIM), lambda i, j: (i, j, 0))]
    return pl.pallas_call(
        _mla_in_kernel,
        grid=(b, nt),
        in_specs=[pl.BlockSpec((1, tm, d), lambda i, j: (i, j, 0)),
                  pl.BlockSpec((1, 1, 3 * d), mod_idx),
                  pl.BlockSpec((1, d), const),
                  pl.BlockSpec(win.shape, const),
                  pl.BlockSpec((1, Q_LORA_RANK), const),
                  pl.BlockSpec(wqb.shape, const),
                  pl.BlockSpec((1, KV_LORA_RANK), const),
                  pl.BlockSpec((tm, 128), tbl_idx),
                  pl.BlockSpec((tm, 128), tbl_idx)],
        out_specs=out_specs,
        out_shape=out_shape,
        compiler_params=_params(("parallel", "parallel")),
        name="mla_in_proj",
    )(x, mod, preg, win, qg, wqb, kvg, cos2, sin2)


def _rows8(x, op):
    return op(x.reshape(x.shape[0] // SUBLANES, SUBLANES, x.shape[1]), axis=0)


def _mla_attn_kernel(q_ref, ckv_ref, kr_ref, wkb_ref, wvbt_ref, o_ref, k_scr, vt_scr, s_scr, *, nk, tq):
    nchunk = nk // MXU_DIM
    chunks = [slice(ci * MXU_DIM, (ci + 1) * MXU_DIM) for ci in range(nchunk)]
    for rows in chunks:
        c = ckv_ref[0, rows, :]
        k_scr[rows, 0:128] = jnp.dot(c, wkb_ref[0], preferred_element_type=F32).astype(BF16)
        k_scr[rows, 128:256] = kr_ref[0, rows, :]
        vt_scr[:, rows] = lax.dot_general(wvbt_ref[0], c, _NT, preferred_element_type=F32).astype(BF16)

    def scores(qi, slot):
        q = q_ref[0, 0, pl.ds(pl.multiple_of(qi * tq, tq), tq), :]
        m8 = None
        for rows in chunks:
            st = lax.dot_general(k_scr[rows, :], q, _NT, preferred_element_type=F32)
            s_scr[slot, rows, :] = st
            cm = _rows8(st, jnp.max)
            m8 = cm if m8 is None else jnp.maximum(m8, cm)
        return jnp.max(m8, axis=0, keepdims=True)

    def finish(qi, slot, m):
        l8 = jnp.zeros((SUBLANES, tq), F32)
        acc = jnp.zeros((V_HEAD_DIM, tq), F32)
        for rows in chunks:
            p = jnp.exp2(s_scr[slot, rows, :] - m)
            l8 = l8 + _rows8(p, jnp.sum)
            acc = acc + jnp.dot(vt_scr[:, rows], p.astype(BF16), preferred_element_type=F32)
        l = jnp.sum(l8, axis=0, keepdims=True)
        o_ref[0, pl.ds(pl.multiple_of(qi * tq, tq), tq), :] = (acc / l).T.astype(BF16)

    def body(j, m):
        qi = 2 * j
        m1 = scores(qi + 1, 1)
        finish(qi, 0, m)
        m2 = scores(qi + 2, 0)
        finish(qi + 1, 1, m1)
        return m2

    nq = q_ref.shape[2] // tq
    m_last = scores(0, 0)
    if nq > 1:
        m_last = lax.fori_loop(0, nq // 2 - 1, body, m_last)
        m_odd = scores(nq - 1, 1)
        finish(nq - 2, 0, m_last)
        finish(nq - 1, 1, m_odd)
    else:
        finish(0, 0, m_last)


def _mla_attn(q, ckv_all, kr_all, wkb, wvbt, tq=256):
    b, nh, t, _ = q.shape
    nk = ckv_all.shape[1]
    return pl.pallas_call(
        functools.partial(_mla_attn_kernel, nk=nk, tq=tq),
        grid=(b, nh),
        in_specs=[pl.BlockSpec((1, 1, t, 256), lambda i, h: (i, h, 0, 0)),
                  pl.BlockSpec((1, nk, 128), lambda i, h: (i, 0, 0)),
                  pl.BlockSpec((1, nk, 128), lambda i, h: (i, 0, 0)),
                  pl.BlockSpec((1, 128, 128), lambda i, h: (h, 0, 0)),
                  pl.BlockSpec((1, 128, 128), lambda i, h: (h, 0, 0))],
        out_specs=pl.BlockSpec((1, t, V_HEAD_DIM), lambda i, h: (i, 0, h)),
        out_shape=jax.ShapeDtypeStruct((b, t, MLA_WIDTH), BF16),
        scratch_shapes=[pltpu.VMEM((nk, 256), BF16), pltpu.VMEM((V_HEAD_DIM, nk), BF16),
                        pltpu.VMEM((2, nk, tq), F32)],
        compiler_params=_params(("parallel", "parallel")),
        name="mla_attention",
    )(q, ckv_all, kr_all, wkb, wvbt)


def _out_kernel(o_ref, sg_ref, x_ref, mod_ref, wout_ref, postg_ref, y_ref):
    og = (o_ref[0].astype(F32) * sg_ref[0].astype(F32)).astype(BF16)
    out = jnp.dot(og, wout_ref[...], preferred_element_type=F32)
    gate = mod_ref[0][:, 2 * D_MODEL:]
    y_ref[0] = x_ref[0] + gate * _rms(out, postg_ref[...])


def _out_proj(o, sg, x, mod, mod_row, wout, postg, tm=512):
    b, t, d = x.shape
    w = o.shape[-1]
    mod_idx = (lambda i, j: (i, 0, 0)) if mod_row is None else (lambda i, j: (mod_row, 0, 0))
    const = lambda i, j: (0, 0)
    return pl.pallas_call(
        _out_kernel,
        grid=(b, t // tm),
        in_specs=[pl.BlockSpec((1, tm, w), lambda i, j: (i, j, 0)),
                  pl.BlockSpec((1, tm, w), lambda i, j: (i, j, 0)),
                  pl.BlockSpec((1, tm, d), lambda i, j: (i, j, 0)),
                  pl.BlockSpec((1, 1, 3 * d), mod_idx),
                  pl.BlockSpec(wout.shape, const),
                  pl.BlockSpec((1, d), const)],
        out_specs=pl.BlockSpec((1, tm, d), lambda i, j: (i, j, 0)),
        out_shape=jax.ShapeDtypeStruct((b, t, d), F32),
        compiler_params=_params(("parallel", "parallel")),
        name="out_proj",
    )(o, sg, x, mod, wout, postg)


def _na_in_kernel(x_ref, mod_ref, preg_ref, win_ref, q_ref, k_ref, vt_ref, sg_ref, *state_refs):
    h = _modulated(x_ref[0], mod_ref[0], preg_ref[...]).astype(BF16)
    w = NA_WIDTH
    q = jnp.dot(h, win_ref[:, 0:w], preferred_element_type=F32)
    q_ref[0] = (q * NA_QSCALE).astype(BF16)
    k = jnp.dot(h, win_ref[:, w:2 * w], preferred_element_type=F32)
    k_ref[0] = k.astype(BF16)
    v = jnp.dot(h, win_ref[:, 2 * w:3 * w], preferred_element_type=F32)
    vt_ref[0] = v.T.astype(BF16)
    if state_refs:
        state_refs[0][0] = k
        state_refs[1][0] = v
    g = jnp.dot(h, win_ref[:, 3 * w:4 * w], preferred_element_type=F32)
    sg_ref[0] = _silu(g).astype(BF16)


def _na_in(x, mod, mod_row, preg, win, with_state, tm):
    b, t, d = x.shape
    w = NA_WIDTH
    mod_idx = (lambda i, j: (i, 0, 0)) if mod_row is None else (lambda i, j: (mod_row, 0, 0))
    const = lambda i, j: (0, 0)
    row_blk = pl.BlockSpec((1, tm, w), lambda i, j: (i, j, 0))
    out_shape = [jax.ShapeDtypeStruct((b, t, w), BF16), jax.ShapeDtypeStruct((b, t, w), BF16),
                 jax.ShapeDtypeStruct((b, w, t), BF16), jax.ShapeDtypeStruct((b, t, w), BF16)]
    out_specs = [row_blk, row_blk, pl.BlockSpec((1, w, tm), lambda i, j: (i, 0, j)), row_blk]
    if with_state:
        out_shape += [jax.ShapeDtypeStruct((b, t, w), F32), jax.ShapeDtypeStruct((b, t, w), F32)]
        out_specs += [row_blk, row_blk]
    return pl.pallas_call(
        _na_in_kernel,
        grid=(b, t // tm),
        in_specs=[pl.BlockSpec((1, tm, d), lambda i, j: (i, j, 0)),
                  pl.BlockSpec((1, 1, 3 * d), mod_idx),
                  pl.BlockSpec((1, d), const),
                  pl.BlockSpec(win.shape, const)],
        out_specs=out_specs,
        out_shape=out_shape,
        compiler_params=_params(("parallel", "parallel")),
        name="na_in_proj",
    )(x, mod, preg, win)


def _head_masks(shape):
    lane = lax.broadcasted_iota(jnp.int32, shape, 1)
    return (lane < NA_HEAD_DIM, lane >= NA_HEAD_DIM)


def _na_ctx_kernel(q_ref, k_ref, vt_ref, o_ref):
    q2 = q_ref[0]
    k2 = k_ref[0]
    outs = []
    for hl, msk in enumerate(_head_masks(q2.shape)):
        qm = jnp.where(msk, q2, jnp.zeros_like(q2))
        st = lax.dot_general(k2, qm, _NT, preferred_element_type=F32)
        p = jnp.exp2(st - jnp.max(st, axis=0, keepdims=True))
        l = jnp.sum(p, axis=0, keepdims=True)
        vt = vt_ref[0, hl * NA_HEAD_DIM:(hl + 1) * NA_HEAD_DIM, :]
        outs.append(jnp.dot(vt, p.astype(BF16), preferred_element_type=F32) / l)
    o_ref[0] = jnp.concatenate(outs, axis=0).T.astype(BF16)


def _na_ctx(q, k, vt):
    b, t, w = q.shape
    return pl.pallas_call(
        _na_ctx_kernel,
        grid=(b, w // LANES),
        in_specs=[pl.BlockSpec((1, t, LANES), lambda i, h: (i, 0, h)),
                  pl.BlockSpec((1, t, LANES), lambda i, h: (i, 0, h)),
                  pl.BlockSpec((1, LANES, t), lambda i, h: (i, h, 0))],
        out_specs=pl.BlockSpec((1, t, LANES), lambda i, h: (i, 0, h)),
        out_shape=jax.ShapeDtypeStruct((b, t, w), BF16),
        compiler_params=_params(("parallel", "parallel")),
        name="na_ctx_attention",
    )(q, k, vt)


def _na_lat_kernel(q_ref, k_ref, vt_ref, kc_ref, vct_ref, bias_ref, o_ref, s_scr, ot_scr, *, nblk):
    tq = NA_QROWS * GRID_W
    nwin = NA_WIN_ROWS * GRID_W
    nctx = kc_ref.shape[1]
    masks = _head_masks((tq, LANES))

    def window(blk):
        ws = jnp.clip(blk * NA_QROWS - NA_MAX_ROWS // 2, 0, nblk * NA_QROWS - NA_WIN_ROWS)
        return pl.multiple_of(ws * GRID_W, MXU_DIM)

    def scores(blk, hl):
        kind = jnp.where(blk == 0, 0, jnp.where(blk == nblk - 1, 2, 1))
        q2 = q_ref[0, pl.ds(pl.multiple_of(blk * tq, tq), tq), :]
        qm = jnp.where(masks[hl], q2, jnp.zeros_like(q2))
        s_loc = lax.dot_general(k_ref[0, pl.ds(window(blk), nwin), :], qm, _NT,
                                preferred_element_type=F32) + bias_ref[hl, kind]
        s_ctx = lax.dot_general(kc_ref[0], qm, _NT, preferred_element_type=F32)
        s_scr[hl, 0:nwin, :] = s_loc
        s_scr[hl, nwin:nwin + nctx, :] = s_ctx
        m8 = jnp.maximum(_rows8(s_loc, jnp.max), _rows8(s_ctx, jnp.max))
        return jnp.max(m8, axis=0, keepdims=True)

    def finish(blk, hl, m):
        rows = slice(hl * NA_HEAD_DIM, (hl + 1) * NA_HEAD_DIM)
        p_loc = jnp.exp2(s_scr[hl, 0:nwin, :] - m)
        p_ctx = jnp.exp2(s_scr[hl, nwin:nwin + nctx, :] - m)
        l = jnp.sum(_rows8(p_loc, jnp.sum) + _rows8(p_ctx, jnp.sum), axis=0, keepdims=True)
        ot = (jnp.dot(vt_ref[0, rows, pl.ds(window(blk), nwin)], p_loc.astype(BF16),
                      preferred_element_type=F32)
              + jnp.dot(vct_ref[0, rows, :], p_ctx.astype(BF16), preferred_element_type=F32))
        ot_scr[rows, :] = ot / l

    def store(blk):
        o_ref[0, pl.ds(pl.multiple_of(blk * tq, tq), tq), :] = ot_scr[...].T.astype(BF16)

    def body(blk, m0):
        m1 = scores(blk, 1)
        finish(blk, 0, m0)
        m0_next = scores(blk + 1, 0)
        finish(blk, 1, m1)
        store(blk)
        return m0_next

    m0 = lax.fori_loop(0, nblk - 1, body, scores(0, 0))
    m1 = scores(nblk - 1, 1)
    finish(nblk - 1, 0, m0)
    finish(nblk - 1, 1, m1)
    store(nblk - 1)


def _na_lat(q, k, vt, kc, vct, bias):
    b, t, w = q.shape
    tq = NA_QROWS * GRID_W
    nblk = t // tq
    nctx = kc.shape[1]
    nwin = NA_WIN_ROWS * GRID_W
    return pl.pallas_call(
        functools.partial(_na_lat_kernel, nblk=nblk),
        grid=(w // LANES, b),
        in_specs=[pl.BlockSpec((1, t, LANES), lambda h, i: (i, 0, h)),
                  pl.BlockSpec((1, t, LANES), lambda h, i: (i, 0, h)),
                  pl.BlockSpec((1, LANES, t), lambda h, i: (i, h, 0)),
                  pl.BlockSpec((1, nctx, LANES), lambda h, i: (i, 0, h)),
                  pl.BlockSpec((1, LANES, nctx), lambda h, i: (i, h, 0)),
                  pl.BlockSpec((2, 3, nwin, tq), lambda h, i: (h, 0, 0, 0))],
        out_specs=pl.BlockSpec((1, t, LANES), lambda h, i: (i, 0, h)),
        out_shape=jax.ShapeDtypeStruct((b, t, w), BF16),
        scratch_shapes=[pltpu.VMEM((2, nwin + nctx, tq), F32), pltpu.VMEM((LANES, tq), F32)],
        compiler_params=_params(("parallel", "parallel")),
        name="na_lat_attention",
    )(q, k, vt, kc, vct, bias)


def _na_bias_tables(rel_bias, rows):
    nblk = rows // NA_QROWS
    cols = np.arange(GRID_W)
    col_start = np.clip(cols - NA_COLS // 2, 0, GRID_W - NA_COLS)
    col_ok = (cols[:, None] >= col_start[None, :]) & (cols[:, None] < col_start[None, :] + NA_COLS)
    dc = np.clip(cols[:, None] - cols[None, :] + NA_COLS - 1, 0, 2 * NA_COLS - 2)
    dr = np.zeros((3, NA_WIN_ROWS, NA_QROWS), np.int32)
    ok = np.zeros((3, NA_WIN_ROWS, NA_QROWS), bool)
    for kind, blk in enumerate((0, 1, nblk - 1)):
        r0 = blk * NA_QROWS
        ws = int(np.clip(r0 - NA_MAX_ROWS // 2, 0, rows - NA_WIN_ROWS))
        for j in range(NA_WIN_ROWS):
            for i in range(NA_QROWS):
                rs = int(np.clip(r0 + i - NA_MAX_ROWS // 2, 0, rows - NA_MAX_ROWS))
                ok[kind, j, i] = rs <= ws + j < rs + NA_MAX_ROWS
                dr[kind, j, i] = np.clip(ws + j - (r0 + i) + NA_MAX_ROWS - 1, 0, 2 * NA_MAX_ROWS - 2)
    nh, n_dr, n_dc = rel_bias.shape
    half = LANES // GRID_W
    return pl.pallas_call(
        functools.partial(_na_bias_kernel, n_dr=n_dr, n_dc=n_dc, dr_idx=dr.tolist(), ok=ok.tolist()),
        grid=(nh,),
        in_specs=[pl.BlockSpec(memory_space=pltpu.SMEM),
                  pl.BlockSpec((GRID_W, LANES), lambda h: (0, 0)),
                  pl.BlockSpec((GRID_W, LANES), lambda h: (0, 0))],
        out_specs=pl.BlockSpec((1, 3, NA_WIN_ROWS * GRID_W, NA_QROWS * GRID_W), lambda h: (h, 0, 0, 0)),
        out_shape=jax.ShapeDtypeStruct((nh, 3, NA_WIN_ROWS * GRID_W, NA_QROWS * GRID_W), F32),
        scratch_shapes=[pltpu.VMEM((n_dr, GRID_W, NA_QROWS * GRID_W), F32)],
        compiler_params=_params(("parallel",)),
        name="na_bias_tables",
    )(rel_bias.reshape(-1).astype(F32),
      jnp.asarray(np.tile(dc, (1, half)), jnp.int32),
      jnp.asarray(np.tile(col_ok, (1, half)), jnp.int32))


def _na_bias_kernel(b_ref, dc_ref, colok_ref, o_ref, tt_scr, *, n_dr, n_dc, dr_idx, ok):
    base = pl.program_id(0) * (n_dr * n_dc)
    dcb = dc_ref[...]
    colok = colok_ref[...] > 0
    for d in range(n_dr):
        acc = jnp.zeros(dcb.shape, F32)
        for e in range(n_dc):
            acc = jnp.where(dcb == e, b_ref[base + d * n_dc + e], acc)
        tile = jnp.where(colok, acc * LOG2E, -jnp.inf)
        tt_scr[d] = jnp.concatenate([tile] * (tt_scr.shape[2] // LANES), axis=1)
    shape = tt_scr.shape[1:]
    qrow = lax.broadcasted_iota(jnp.int32, shape, 1) // GRID_W
    neg = jnp.full(shape, -jnp.inf, F32)
    for kind in range(3):
        for j in range(NA_WIN_ROWS):
            band = neg
            for i in range(NA_QROWS):
                if ok[kind][j][i]:
                    band = jnp.where(qrow == i, tt_scr[dr_idx[kind][j][i]], band)
            o_ref[0, kind, j * GRID_W:(j + 1) * GRID_W, :] = band


def _rotate_half_cols(w):
    w4 = w.reshape(w.shape[:-1] + (2, 2, ROPE_AXIS_FREQS))
    return jnp.stack([-w4[..., 1, :], w4[..., 0, :]], axis=-2).reshape(w.shape)


def _rope_tables(n):
    t = jnp.arange(n)
    pos = jnp.stack([t // GRID_W, t % GRID_W], axis=-1).astype(F32)
    inv = ROPE_THETA ** (-jnp.arange(ROPE_AXIS_FREQS, dtype=F32) / ROPE_AXIS_FREQS)
    ang = pos[:, :, None] * inv
    cos, sin = jnp.cos(ang), jnp.sin(ang)
    zeros = jnp.zeros((n, LANES - QK_ROPE_DIM), F32)
    cos2 = jnp.concatenate([cos[:, 0], cos[:, 0], cos[:, 1], cos[:, 1], zeros], axis=-1)
    sin2 = jnp.concatenate([sin[:, 0], sin[:, 0], sin[:, 1], sin[:, 1], zeros], axis=-1)
    return cos2.astype(F32), sin2.astype(F32)


def kernel(x_prompt, x_sample, cache_mla_ckv, cache_mla_krope, cache_na_k, cache_na_v, c, c_ctx,
           w_ada, b_ada, pre_norm_g, post_norm_g, mla_w_in, mla_q_norm_g, mla_w_qb, mla_kv_norm_g,
           mla_w_kvb, mla_w_out, na_w_in, na_rel_bias, na_w_out):
    bp, tp, d = x_prompt.shape
    bs, ts, _ = x_sample.shape
    tm = 512
    tmp = min(tm, tp)

    cond = jnp.zeros((MOD_ROWS, d), F32).at[:bs].set(c).at[CTX_ROW].set(c_ctx)
    mod = _modulation(cond, w_ada, b_ada)
    mod = mod.reshape(mod.shape[0], MOD_ROWS, 1, 3 * d)

    w_in = mla_w_in[0]
    n_small = Q_LORA_RANK + KV_LORA_RANK + QK_ROPE_DIM
    w_in_ext = jnp.concatenate(
        [w_in[:, :n_small], _rotate_half_cols(w_in[:, n_small - QK_ROPE_DIM:n_small]), w_in[:, n_small:]],
        axis=1).astype(BF16)
    wqb3 = mla_w_qb[0].reshape(Q_LORA_RANK, MLA_HEADS, QK_NOPE_DIM + QK_ROPE_DIM)
    wqb_ext = jnp.concatenate([wqb3, _rotate_half_cols(wqb3[..., QK_NOPE_DIM:])], axis=-1)
    wqb_ext = wqb_ext.reshape(Q_LORA_RANK, MLA_HEADS * 256).astype(BF16)
    wkvb3 = mla_w_kvb[0].reshape(KV_LORA_RANK, MLA_HEADS, QK_NOPE_DIM + V_HEAD_DIM)
    wkb = wkvb3[..., :QK_NOPE_DIM].transpose(1, 0, 2).astype(BF16)
    wvbt = wkvb3[..., QK_NOPE_DIM:].transpose(1, 2, 0).astype(BF16)
    preg0 = pre_norm_g[0].reshape(1, d)
    postg0 = post_norm_g[0].reshape(1, d)
    qg = mla_q_norm_g[0].reshape(1, Q_LORA_RANK)
    kvg = mla_kv_norm_g[0].reshape(1, KV_LORA_RANK)
    cos_s, sin_s = _rope_tables(ts)
    pad = jnp.zeros((tmp, LANES - QK_ROPE_DIM), F32)
    cos_p = jnp.concatenate([jnp.ones((tmp, QK_ROPE_DIM), F32), pad], axis=-1)
    sin_p = jnp.zeros((tmp, LANES), F32)
    w_out0 = mla_w_out[0].astype(BF16)

    qp, ckvb_p, krb_p, sg_p, ckv_p, kr_p = _mla_in(
        x_prompt, mod[0], CTX_ROW, preg0, w_in_ext, qg, wqb_ext, kvg, cos_p, sin_p, True, tmp)
    qs, ckvb_s, krb_s, sg_s = _mla_in(
        x_sample, mod[0], None, preg0, w_in_ext, qg, wqb_ext, kvg, cos_s, sin_s, False, tm)

    op = _mla_attn(qp, ckvb_p, krb_p, wkb, wvbt)
    ckv_all = jnp.concatenate([cache_mla_ckv[:, 0].astype(BF16), ckvb_s], axis=1)
    kr_ctx = jnp.pad(cache_mla_krope[:, 0], ((0, 0), (0, 0), (0, LANES - QK_ROPE_DIM))).astype(BF16)
    kr_all = jnp.concatenate([kr_ctx, krb_s], axis=1)
    osm = _mla_attn(qs, ckv_all, kr_all, wkb, wvbt)

    xp = _out_proj(op, sg_p, x_prompt, mod[0], CTX_ROW, w_out0, postg0, tmp)
    xs = _out_proj(osm, sg_s, x_sample, mod[0], None, w_out0, postg0, tm)

    preg1 = pre_norm_g[1].reshape(1, d)
    postg1 = post_norm_g[1].reshape(1, d)
    na_win = na_w_in[0].astype(BF16)
    w_out1 = na_w_out[0].astype(BF16)
    q1p, k1p, vt1p, sg1p, k_state, v_state = _na_in(xp, mod[1], CTX_ROW, preg1, na_win, True, tmp)
    q1s, k1s, vt1s, sg1s = _na_in(xs, mod[1], None, preg1, na_win, False, tm)
    o1p = _na_ctx(q1p, k1p, vt1p)
    npast = cache_na_k.shape[2]
    kc = cache_na_k[:, 0].reshape(bs, npast, NA_WIDTH).astype(BF16)
    vct = cache_na_v[:, 0].reshape(bs, npast, NA_WIDTH).transpose(0, 2, 1).astype(BF16)
    bias = _na_bias_tables(na_rel_bias[0], ts // GRID_W)
    o1s = _na_lat(q1s, k1s, vt1s, kc, vct, bias)
    yp = _out_proj(o1p, sg1p, xp, mod[1], CTX_ROW, w_out1, postg1, tmp)
    ys = _out_proj(o1s, sg1s, xs, mod[1], None, w_out1, postg1, tm)

    return (yp, ys,
            ckv_p.reshape(bp, 1, tp, KV_LORA_RANK),
            kr_p.reshape(bp, 1, tp, QK_ROPE_DIM),
            k_state.reshape(bp, 1, tp, NA_HEADS, NA_HEAD_DIM),
            v_state.reshape(bp, 1, tp, NA_HEADS, NA_HEAD_DIM))
```

```python
import functools
import math

import numpy as np
import jax
import jax.numpy as jnp
from jax import lax
from jax.experimental import pallas as pl
from jax.experimental.pallas import tpu as pltpu

F32 = jnp.float32
BF16 = jnp.bfloat16

D_MODEL = 1024
GRID_W = 64
MLA_HEADS = 16
Q_LORA_RANK = 256
KV_LORA_RANK = 128
QK_NOPE_DIM = 128
QK_ROPE_DIM = 64
V_HEAD_DIM = 128
MLA_WIDTH = MLA_HEADS * V_HEAD_DIM
ROPE_AXIS_FREQS = QK_ROPE_DIM // 4
ROPE_THETA = 10000.0
NA_HEADS = 16
NA_HEAD_DIM = 64
NA_WIDTH = NA_HEADS * NA_HEAD_DIM
NA_MAX_ROWS = 8
NA_COLS = 16
EPS = 1e-6
LOG2E = math.log2(math.e)
MLA_QSCALE = (QK_NOPE_DIM + QK_ROPE_DIM) ** -0.5 * LOG2E
NA_QSCALE = NA_HEAD_DIM ** -0.5 * LOG2E

LANES = 128
SUBLANES = 8
BF16_ROWS = 16
MXU_DIM = 256
MOD_ROWS = 16
CTX_ROW = 8
VMEM_LIMIT = 48 * 1024 * 1024

NA_QROWS = 4
NA_WIN_ROWS = NA_QROWS + NA_MAX_ROWS

_NT = (((1,), (1,)), ((), ()))


def _params(sem, vmem=VMEM_LIMIT):
    return pltpu.CompilerParams(dimension_semantics=sem, vmem_limit_bytes=vmem)


def _rms(x, g):
    return x * lax.rsqrt(jnp.mean(x * x, axis=-1, keepdims=True) + EPS) * g


def _silu(x):
    return x * jax.nn.sigmoid(x)


def _mod_kernel(cond_ref, w_ref, b_ref, o_ref):
    o_ref[0] = jnp.dot(_silu(cond_ref[...]), w_ref[0], preferred_element_type=F32) + b_ref[0]


def _modulation(cond, w_ada, b_ada):
    depth, d, n = w_ada.shape
    bn = 768
    return pl.pallas_call(
        _mod_kernel,
        grid=(depth, n // bn),
        in_specs=[pl.BlockSpec((MOD_ROWS, d), lambda i, j: (0, 0)),
                  pl.BlockSpec((1, d, bn), lambda i, j: (i, 0, j)),
                  pl.BlockSpec((1, 1, bn), lambda i, j: (i, 0, j))],
        out_specs=pl.BlockSpec((1, MOD_ROWS, bn), lambda i, j: (i, 0, j)),
        out_shape=jax.ShapeDtypeStruct((depth, MOD_ROWS, n), F32),
        compiler_params=_params(("parallel", "parallel")),
        name="modulation",
    )(cond, w_ada, b_ada.reshape(depth, 1, n))


def _modulated(x, mod, g):
    return _rms(x, g) * (1.0 + mod[:, D_MODEL:2 * D_MODEL]) + mod[:, :D_MODEL]


def _mla_in_kernel(x_ref, mod_ref, preg_ref, win_ref, qg_ref, wqb_ref, kvg_ref, cos_ref, sin_ref,
                   q_ref, ckvb_ref, krb_ref, sg_ref, *state_refs):
    h = _modulated(x_ref[0], mod_ref[0], preg_ref[...]).astype(BF16)
    ya = jnp.dot(h, win_ref[:, :512], preferred_element_type=F32)
    ckv = _rms(ya[:, 256:384], kvg_ref[...])
    kr2 = ya[:, 384:512]
    cos2 = cos_ref[...]
    sin2 = sin_ref[...]
    ckvb_ref[0] = ckv.astype(BF16)
    krb_ref[0] = (kr2 * cos2 + pltpu.roll(kr2, 64, 1) * sin2).astype(BF16)
    if state_refs:
        state_refs[0][0] = ckv
        state_refs[1][0] = kr2[:, :QK_ROPE_DIM]
    qn = _rms(ya[:, :256], qg_ref[...]).astype(BF16)
    for hh in range(MLA_HEADS):
        yq = jnp.dot(qn, wqb_ref[:, hh * 256:(hh + 1) * 256], preferred_element_type=F32)
        sec = yq[:, 128:]
        q_ref[0, hh, :, 0:128] = (yq[:, :128] * MLA_QSCALE).astype(BF16)
        q_ref[0, hh, :, 128:256] = ((sec * cos2 + pltpu.roll(sec, 64, 1) * sin2) * MLA_QSCALE).astype(BF16)
    for c in range(MLA_WIDTH // 512):
        g = jnp.dot(h, win_ref[:, 512 + c * 512:1024 + c * 512], preferred_element_type=F32)
        sg_ref[0, :, c * 512:(c + 1) * 512] = _silu(g).astype(BF16)


def _mla_in(x, mod, mod_row, preg, win, qg, wqb, kvg, cos2, sin2, with_state, tm=512):
    b, t, d = x.shape
    nt = t // tm
    rope_blocks = cos2.shape[0] // tm
    tbl_idx = (lambda i, j: (j, 0)) if rope_blocks > 1 else (lambda i, j: (0, 0))
    mod_idx = (lambda i, j: (i, 0, 0)) if mod_row is None else (lambda i, j: (mod_row, 0, 0))
    const = lambda i, j: (0, 0)
    out_shape = [jax.ShapeDtypeStruct((b, MLA_HEADS, t, 256), BF16),
                 jax.ShapeDtypeStruct((b, t, 128), BF16),
                 jax.ShapeDtypeStruct((b, t, 128), BF16),
                 jax.ShapeDtypeStruct((b, t, MLA_WIDTH), BF16)]
    out_specs = [pl.BlockSpec((1, MLA_HEADS, tm, 256), lambda i, j: (i, 0, j, 0)),
                 pl.BlockSpec((1, tm, 128), lambda i, j: (i, j, 0)),
                 pl.BlockSpec((1, tm, 128), lambda i, j: (i, j, 0)),
                 pl.BlockSpec((1, tm, MLA_WIDTH), lambda i, j: (i, j, 0))]
    if with_state:
        out_shape += [jax.ShapeDtypeStruct((b, t, KV_LORA_RANK), F32),
                      jax.ShapeDtypeStruct((b, t, QK_ROPE_DIM), F32)]
        out_specs += [pl.BlockSpec((1, tm, KV_LORA_RANK), lambda i, j: (i, j, 0)),
                      pl.BlockSpec((1, tm, QK_ROPE_DIM), lambda i, j: (i, j, 0))]
    return pl.pallas_call(
        _mla_in_kernel,
        grid=(b, nt),
        in_specs=[pl.BlockSpec((1, tm, d), lambda i, j: (i, j, 0)),
                  pl.BlockSpec((1, 1, 3 * d), mod_idx),
                  pl.BlockSpec((1, d), const),
                  pl.BlockSpec(win.shape, const),
                  pl.BlockSpec((1, Q_LORA_RANK), const),
                  pl.BlockSpec(wqb.shape, const),
                  pl.BlockSpec((1, KV_LORA_RANK), const),
                  pl.BlockSpec((tm, 128), tbl_idx),
                  pl.BlockSpec((tm, 128), tbl_idx)],
        out_specs=out_specs,
        out_shape=out_shape,
        compiler_params=_params(("parallel", "parallel")),
        name="mla_in_proj",
    )(x, mod, preg, win, qg, wqb, kvg, cos2, sin2)


def _rows8(x, op):
    return op(x.reshape(x.shape[0] // SUBLANES, SUBLANES, x.shape[1]), axis=0)


def _pv_with_colsum(vts, ps):
    acc = None
    for vt, p in zip(vts, ps):
        lhs = jnp.concatenate([vt, jnp.ones((BF16_ROWS, vt.shape[1]), BF16)], axis=0)
        part = jnp.dot(lhs, p, preferred_element_type=F32)
        acc = part if acc is None else acc + part
    nd = vts[0].shape[0]
    return acc[:nd], acc[nd:nd + 1]


def _mla_attn_kernel(q_ref, cckv_ref, ckr_ref, ckv_ref, kr_ref, wkb_ref, wvbt_ref, o_ref,
                     k_scr, vt_scr, s_scr, *, tq, unroll):
    nctx = cckv_ref.shape[2]
    nk = nctx + ckv_ref.shape[1]
    chunks = [slice(lo, lo + MXU_DIM) for lo in range(0, nk, MXU_DIM)]
    for rows in chunks:
        if rows.start < nctx:
            c = cckv_ref[0, 0, rows, :].astype(BF16)
            k_scr[rows, 128:128 + QK_ROPE_DIM] = ckr_ref[0, 0, rows, :].astype(BF16)
            k_scr[rows, 128 + QK_ROPE_DIM:256] = jnp.zeros((MXU_DIM, 128 - QK_ROPE_DIM), BF16)
        else:
            lat = slice(rows.start - nctx, rows.stop - nctx)
            c = ckv_ref[0, lat, :]
            k_scr[rows, 128:256] = kr_ref[0, lat, :]
        k_scr[rows, 0:128] = jnp.dot(c, wkb_ref[0], preferred_element_type=F32).astype(BF16)
        vt_scr[:, rows] = lax.dot_general(wvbt_ref[0], c, _NT, preferred_element_type=F32).astype(BF16)

    def scores(qi, slot):
        q = q_ref[0, 0, pl.ds(pl.multiple_of(qi * tq, tq), tq), :]
        m8 = None
        for rows in chunks:
            st = lax.dot_general(k_scr[rows, :], q, _NT, preferred_element_type=F32)
            s_scr[slot, rows, :] = st
            cm = _rows8(st, jnp.max)
            m8 = cm if m8 is None else jnp.maximum(m8, cm)
        return jnp.max(m8, axis=0, keepdims=True)

    def finish(qi, slot, m):
        l8 = jnp.zeros((SUBLANES, tq), F32)
        acc = jnp.zeros((V_HEAD_DIM, tq), F32)
        for rows in chunks:
            p = jnp.exp2(s_scr[slot, rows, :] - m)
            l8 = l8 + _rows8(p, jnp.sum)
            acc = acc + jnp.dot(vt_scr[:, rows], p.astype(BF16), preferred_element_type=F32)
        l = jnp.sum(l8, axis=0, keepdims=True)
        o_ref[0, pl.ds(pl.multiple_of(qi * tq, tq), tq), :] = (acc / l).T.astype(BF16)

    def run(q0, n, m):
        for i in range(n):
            m_next = scores(q0 + i + 1, (i + 1) % 2)
            finish(q0 + i, i % 2, m)
            m = m_next
        return m

    nq = q_ref.shape[2] // tq
    trips = nq // unroll - 1
    m = lax.fori_loop(0, trips, lambda j, m: run(unroll * j, unroll, m), scores(0, 0))
    m = run(trips * unroll, unroll - 1, m)
    finish(nq - 1, (unroll - 1) % 2, m)


def _mla_attn(q, cache_ckv, cache_kr, layer, ckv, kr, wkb, wvbt, tq=256, unroll=4):
    b, nh, t, _ = q.shape
    nctx = cache_ckv.shape[2]
    nk = nctx + t
    assert nctx % MXU_DIM == 0 and t % (unroll * tq) == 0 and unroll % 2 == 0
    return pl.pallas_call(
        functools.partial(_mla_attn_kernel, tq=tq, unroll=unroll),
        grid=(b, nh),
        in_specs=[pl.BlockSpec((1, 1, t, 256), lambda i, h: (i, h, 0, 0)),
                  pl.BlockSpec((1, 1, nctx, KV_LORA_RANK), lambda i, h: (i, layer, 0, 0)),
                  pl.BlockSpec((1, 1, nctx, QK_ROPE_DIM), lambda i, h: (i, layer, 0, 0)),
                  pl.BlockSpec((1, t, 128), lambda i, h: (i, 0, 0)),
                  pl.BlockSpec((1, t, 128), lambda i, h: (i, 0, 0)),
                  pl.BlockSpec((1, 128, 128), lambda i, h: (h, 0, 0)),
                  pl.BlockSpec((1, 128, 128), lambda i, h: (h, 0, 0))],
        out_specs=pl.BlockSpec((1, t, V_HEAD_DIM), lambda i, h: (i, 0, h)),
        out_shape=jax.ShapeDtypeStruct((b, t, MLA_WIDTH), BF16),
        scratch_shapes=[pltpu.VMEM((nk, 256), BF16), pltpu.VMEM((V_HEAD_DIM, nk), BF16),
                        pltpu.VMEM((2, nk, tq), F32)],
        compiler_params=_params(("parallel", "parallel")),
        name="mla_attention",
    )(q, cache_ckv, cache_kr, ckv, kr, wkb, wvbt)


def _mla_ctx_kernel(q_ref, ckv_ref, kr_ref, wkb_ref, wvbt_ref, o_ref):
    c = ckv_ref[0]
    kr = kr_ref[0]
    nh = q_ref.shape[1]

    def expand(h):
        k_nope = jnp.dot(c, wkb_ref[h], preferred_element_type=F32).astype(BF16)
        vt = lax.dot_general(wvbt_ref[h], c, _NT, preferred_element_type=F32).astype(BF16)
        return jnp.concatenate([k_nope, kr], axis=1), vt

    def scores(h, k):
        st = lax.dot_general(k, q_ref[0, h], _NT, preferred_element_type=F32)
        return st, jnp.max(_rows8(st, jnp.max), axis=0, keepdims=True)

    def finish(h, st, m, vt):
        p = jnp.exp2(st - m)
        l = jnp.sum(_rows8(p, jnp.sum), axis=0, keepdims=True)
        ot = jnp.dot(vt, p.astype(BF16), preferred_element_type=F32) / l
        o_ref[0, :, h * V_HEAD_DIM:(h + 1) * V_HEAD_DIM] = ot.T.astype(BF16)

    kv = {0: expand(0)}
    if nh > 1:
        kv[1] = expand(1)
    sm = {0: scores(0, kv[0][0])}
    for h in range(nh):
        if h + 2 < nh:
            kv[h + 2] = expand(h + 2)
        if h + 1 < nh:
            sm[h + 1] = scores(h + 1, kv[h + 1][0])
        finish(h, *sm.pop(h), kv.pop(h)[1])


def _mla_ctx_attn(q, ckv, kr, wkb, wvbt):
    b, nh, t, _ = q.shape
    const3 = lambda i: (0, 0, 0)
    return pl.pallas_call(
        _mla_ctx_kernel,
        grid=(b,),
        in_specs=[pl.BlockSpec((1, nh, t, 256), lambda i: (i, 0, 0, 0)),
                  pl.BlockSpec((1, t, 128), lambda i: (i, 0, 0)),
                  pl.BlockSpec((1, t, 128), lambda i: (i, 0, 0)),
                  pl.BlockSpec(wkb.shape, const3),
                  pl.BlockSpec(wvbt.shape, const3)],
        out_specs=pl.BlockSpec((1, t, MLA_WIDTH), lambda i: (i, 0, 0)),
        out_shape=jax.ShapeDtypeStruct((b, t, MLA_WIDTH), BF16),
        compiler_params=_params(("parallel",)),
        name="mla_ctx_attention",
    )(q, ckv, kr, wkb, wvbt)


def _out_kernel(o_ref, sg_ref, x_ref, mod_ref, wout_ref, postg_ref, y_ref):
    og = (o_ref[0].astype(F32) * sg_ref[0].astype(F32)).astype(BF16)
    out = jnp.dot(og, wout_ref[...], preferred_element_type=F32)
    gate = mod_ref[0][:, 2 * D_MODEL:]
    y_ref[0] = x_ref[0] + gate * _rms(out, postg_ref[...])


def _out_proj(o, sg, x, mod, mod_row, wout, postg, tm=512):
    b, t, d = x.shape
    w = o.shape[-1]
    mod_idx = (lambda i, j: (i, 0, 0)) if mod_row is None else (lambda i, j: (mod_row, 0, 0))
    const = lambda i, j: (0, 0)
    return pl.pallas_call(
        _out_kernel,
        grid=(b, t // tm),
        in_specs=[pl.BlockSpec((1, tm, w), lambda i, j: (i, j, 0)),
                  pl.BlockSpec((1, tm, w), lambda i, j: (i, j, 0)),
                  pl.BlockSpec((1, tm, d), lambda i, j: (i, j, 0)),
                  pl.BlockSpec((1, 1, 3 * d), mod_idx),
                  pl.BlockSpec(wout.shape, const),
                  pl.BlockSpec((1, d), const)],
        out_specs=pl.BlockSpec((1, tm, d), lambda i, j: (i, j, 0)),
        out_shape=jax.ShapeDtypeStruct((b, t, d), F32),
        compiler_params=_params(("parallel", "parallel")),
        name="out_proj",
    )(o, sg, x, mod, wout, postg)


def _na_in_kernel(x_ref, mod_ref, preg_ref, win_ref, q_ref, k_ref, vt_ref, sg_ref, *state_refs):
    h = _modulated(x_ref[0], mod_ref[0], preg_ref[...]).astype(BF16)
    w = NA_WIDTH
    q = jnp.dot(h, win_ref[:, 0:w], preferred_element_type=F32)
    q_ref[0] = (q * NA_QSCALE).astype(BF16)
    k = jnp.dot(h, win_ref[:, w:2 * w], preferred_element_type=F32)
    k_ref[0] = k.astype(BF16)
    v = jnp.dot(h, win_ref[:, 2 * w:3 * w], preferred_element_type=F32)
    vt = v.T.astype(BF16)
    for ci in range(vt_ref.shape[1]):
        vt_ref[0, ci] = vt[:, ci * MXU_DIM:(ci + 1) * MXU_DIM]
    if state_refs:
        state_refs[0][0] = k
        state_refs[1][0] = v
    g = jnp.dot(h, win_ref[:, 3 * w:4 * w], preferred_element_type=F32)
    sg_ref[0] = _silu(g).astype(BF16)


def _na_in(x, mod, mod_row, preg, win, with_state, tm):
    b, t, d = x.shape
    w = NA_WIDTH
    mod_idx = (lambda i, j: (i, 0, 0)) if mod_row is None else (lambda i, j: (mod_row, 0, 0))
    const = lambda i, j: (0, 0)
    row_blk = pl.BlockSpec((1, tm, w), lambda i, j: (i, j, 0))
    out_shape = [jax.ShapeDtypeStruct((b, t, w), BF16), jax.ShapeDtypeStruct((b, t, w), BF16),
                 jax.ShapeDtypeStruct((b, t // MXU_DIM, w, MXU_DIM), BF16),
                 jax.ShapeDtypeStruct((b, t, w), BF16)]
    out_specs = [row_blk, row_blk,
                 pl.BlockSpec((1, tm // MXU_DIM, w, MXU_DIM), lambda i, j: (i, j, 0, 0)), row_blk]
    if with_state:
        out_shape += [jax.ShapeDtypeStruct((b, t, w), F32), jax.ShapeDtypeStruct((b, t, w), F32)]
        out_specs += [row_blk, row_blk]
    return pl.pallas_call(
        _na_in_kernel,
        grid=(b, t // tm),
        in_specs=[pl.BlockSpec((1, tm, d), lambda i, j: (i, j, 0)),
                  pl.BlockSpec((1, 1, 3 * d), mod_idx),
                  pl.BlockSpec((1, d), const),
                  pl.BlockSpec(win.shape, const)],
        out_specs=out_specs,
        out_shape=out_shape,
        compiler_params=_params(("parallel", "parallel")),
        name="na_in_proj",
    )(x, mod, preg, win)


def _head_masks(shape):
    lane = lax.broadcasted_iota(jnp.int32, shape, 1)
    return (lane < NA_HEAD_DIM, lane >= NA_HEAD_DIM)


def _na_ctx_kernel(q_ref, k_ref, vt_ref, o_ref):
    masks = _head_masks((q_ref.shape[1], LANES))
    nhead = q_ref.shape[2] // NA_HEAD_DIM

    def scores(h):
        cols = slice(h // 2 * LANES, (h // 2 + 1) * LANES)
        q2 = q_ref[0, :, cols]
        qm = jnp.where(masks[h % 2], q2, jnp.zeros_like(q2))
        st = lax.dot_general(k_ref[0, :, cols], qm, _NT, preferred_element_type=F32)
        return st, jnp.max(_rows8(st, jnp.max), axis=0, keepdims=True)

    def finish(h, st, m):
        p = jnp.exp2(st - m).astype(BF16)
        ot, l = _pv_with_colsum([vt_ref[0, 0, h * NA_HEAD_DIM:(h + 1) * NA_HEAD_DIM, :]], [p])
        return ot / l

    nxt = scores(0)
    outs = []
    for h in range(nhead):
        cur, nxt = nxt, (scores(h + 1) if h + 1 < nhead else None)
        outs.append(finish(h, *cur))
        if h % 2:
            cols = slice(h // 2 * LANES, (h // 2 + 1) * LANES)
            o_ref[0, :, cols] = jnp.concatenate(outs, axis=0).T.astype(BF16)
            outs = []


def _na_ctx(q, k, vt):
    b, t, w = q.shape
    return pl.pallas_call(
        _na_ctx_kernel,
        grid=(b,),
        in_specs=[pl.BlockSpec((1, t, w), lambda i: (i, 0, 0)),
                  pl.BlockSpec((1, t, w), lambda i: (i, 0, 0)),
                  pl.BlockSpec((1, t // MXU_DIM, w, MXU_DIM), lambda i: (i, 0, 0, 0))],
        out_specs=pl.BlockSpec((1, t, w), lambda i: (i, 0, 0)),
        out_shape=jax.ShapeDtypeStruct((b, t, w), BF16),
        compiler_params=_params(("parallel",)),
        name="na_ctx_attention",
    )(q, k, vt)


def _na_lat_kernel(q_ref, k_ref, vt_ref, kc_ref, vct_ref, bias_ref, o_ref, s_scr, ot_scr, *, nblk):
    tq = NA_QROWS * GRID_W
    nwin = NA_WIN_ROWS * GRID_W
    nctx = kc_ref.shape[1]
    masks = _head_masks((tq, LANES))

    def window(blk):
        ws = jnp.clip(blk * NA_QROWS - NA_MAX_ROWS // 2, 0, nblk * NA_QROWS - NA_WIN_ROWS)
        return pl.multiple_of(ws * GRID_W, MXU_DIM)

    def scores(blk, par):
        kind = jnp.where(blk == 0, 0, jnp.where(blk == nblk - 1, 2, 1))
        q2 = q_ref[0, pl.ds(pl.multiple_of(blk * tq, tq), tq), :]
        kwin = k_ref[0, pl.ds(window(blk), nwin), :]
        ms = []
        for hl in range(2):
            qm = jnp.where(masks[hl], q2, jnp.zeros_like(q2))
            s_loc = lax.dot_general(kwin, qm, _NT, preferred_element_type=F32) + bias_ref[hl, kind]
            s_ctx = lax.dot_general(kc_ref[0], qm, _NT, preferred_element_type=F32)
            s_scr[2 * par + hl, 0:nwin, :] = s_loc
            s_scr[2 * par + hl, nwin:nwin + nctx, :] = s_ctx
            m8 = jnp.maximum(_rows8(s_loc, jnp.max), _rows8(s_ctx, jnp.max))
            ms.append(jnp.max(m8, axis=0, keepdims=True))
        return tuple(ms)

    def finish(blk, par, ms):
        c0 = window(blk) // MXU_DIM
        for hl in range(2):
            rows = slice(hl * NA_HEAD_DIM, (hl + 1) * NA_HEAD_DIM)
            vts = [vt_ref[0, c0 + ci, rows, :] for ci in range(nwin // MXU_DIM)] + [vct_ref[0, rows, :]]
            edges = list(range(0, nwin + 1, MXU_DIM)) + [nwin + nctx]
            ps = [jnp.exp2(s_scr[2 * par + hl, lo:hi, :] - ms[hl]).astype(BF16)
                  for lo, hi in zip(edges[:-1], edges[1:])]
            ot, l = _pv_with_colsum(vts, ps)
            ot_scr[par, rows, :] = ot / l
        o_ref[0, pl.ds(pl.multiple_of(blk * tq, tq), tq), :] = ot_scr[par].T.astype(BF16)

    def body(j, ms):
        blk = 2 * j
        ms1 = scores(blk + 1, 1)
        finish(blk, 0, ms)
        ms2 = scores(blk + 2, 0)
        finish(blk + 1, 1, ms1)
        return ms2

    ms = lax.fori_loop(0, nblk // 2 - 1, body, scores(0, 0))
    ms_last = scores(nblk - 1, 1)
    finish(nblk - 2, 0, ms)
    finish(nblk - 1, 1, ms_last)


def _na_lat(q, k, vt, kc, vct, bias):
    b, t, w = q.shape
    tq = NA_QROWS * GRID_W
    nblk = t // tq
    nctx = kc.shape[1]
    nwin = NA_WIN_ROWS * GRID_W
    return pl.pallas_call(
        functools.partial(_na_lat_kernel, nblk=nblk),
        grid=(w // LANES, b),
        in_specs=[pl.BlockSpec((1, t, LANES), lambda h, i: (i, 0, h)),
                  pl.BlockSpec((1, t, LANES), lambda h, i: (i, 0, h)),
                  pl.BlockSpec((1, t // MXU_DIM, LANES, MXU_DIM), lambda h, i: (i, 0, h, 0)),
                  pl.BlockSpec((1, nctx, LANES), lambda h, i: (i, 0, h)),
                  pl.BlockSpec((1, LANES, nctx), lambda h, i: (i, h, 0)),
                  pl.BlockSpec((2, 3, nwin, tq), lambda h, i: (h, 0, 0, 0))],
        out_specs=pl.BlockSpec((1, t, LANES), lambda h, i: (i, 0, h)),
        out_shape=jax.ShapeDtypeStruct((b, t, w), BF16),
        scratch_shapes=[pltpu.VMEM((4, nwin + nctx, tq), F32), pltpu.VMEM((2, LANES, tq), F32)],
        compiler_params=_params(("parallel", "parallel")),
        name="na_lat_attention",
    )(q, k, vt, kc, vct, bias)


def _na_bias_tables(rel_bias, rows):
    nblk = rows // NA_QROWS
    cols = np.arange(GRID_W)
    col_start = np.clip(cols - NA_COLS // 2, 0, GRID_W - NA_COLS)
    col_ok = (cols[:, None] >= col_start[None, :]) & (cols[:, None] < col_start[None, :] + NA_COLS)
    dc = np.clip(cols[:, None] - cols[None, :] + NA_COLS - 1, 0, 2 * NA_COLS - 2)
    dr = np.zeros((3, NA_WIN_ROWS, NA_QROWS), np.int32)
    ok = np.zeros((3, NA_WIN_ROWS, NA_QROWS), bool)
    for kind, blk in enumerate((0, 1, nblk - 1)):
        r0 = blk * NA_QROWS
        ws = int(np.clip(r0 - NA_MAX_ROWS // 2, 0, rows - NA_WIN_ROWS))
        for j in range(NA_WIN_ROWS):
            for i in range(NA_QROWS):
                rs = int(np.clip(r0 + i - NA_MAX_ROWS // 2, 0, rows - NA_MAX_ROWS))
                ok[kind, j, i] = rs <= ws + j < rs + NA_MAX_ROWS
                dr[kind, j, i] = np.clip(ws + j - (r0 + i) + NA_MAX_ROWS - 1, 0, 2 * NA_MAX_ROWS - 2)
    nh, n_dr, n_dc = rel_bias.shape
    half = LANES // GRID_W
    return pl.pallas_call(
        functools.partial(_na_bias_kernel, n_dr=n_dr, n_dc=n_dc, dr_idx=dr.tolist(), ok=ok.tolist()),
        grid=(nh,),
        in_specs=[pl.BlockSpec(memory_space=pltpu.SMEM),
                  pl.BlockSpec((GRID_W, LANES), lambda h: (0, 0)),
                  pl.BlockSpec((GRID_W, LANES), lambda h: (0, 0))],
        out_specs=pl.BlockSpec((1, 3, NA_WIN_ROWS * GRID_W, NA_QROWS * GRID_W), lambda h: (h, 0, 0, 0)),
        out_shape=jax.ShapeDtypeStruct((nh, 3, NA_WIN_ROWS * GRID_W, NA_QROWS * GRID_W), F32),
        scratch_shapes=[pltpu.VMEM((n_dr, GRID_W, NA_QROWS * GRID_W), F32)],
        compiler_params=_params(("parallel",)),
        name="na_bias_tables",
    )(rel_bias.reshape(-1).astype(F32),
      jnp.asarray(np.tile(dc, (1, half)), jnp.int32),
      jnp.asarray(np.tile(col_ok, (1, half)), jnp.int32))


def _na_bias_kernel(b_ref, dc_ref, colok_ref, o_ref, tt_scr, *, n_dr, n_dc, dr_idx, ok):
    base = pl.program_id(0) * (n_dr * n_dc)
    dcb = dc_ref[...]
    colok = colok_ref[...] > 0
    for d in range(n_dr):
        acc = jnp.zeros(dcb.shape, F32)
        for e in range(n_dc):
            acc = jnp.where(dcb == e, b_ref[base + d * n_dc + e], acc)
        tile = jnp.where(colok, acc * LOG2E, -jnp.inf)
        tt_scr[d] = jnp.concatenate([tile] * (tt_scr.shape[2] // LANES), axis=1)
    shape = tt_scr.shape[1:]
    qrow = lax.broadcasted_iota(jnp.int32, shape, 1) // GRID_W
    neg = jnp.full(shape, -jnp.inf, F32)
    for kind in range(3):
        for j in range(NA_WIN_ROWS):
            band = neg
            for i in range(NA_QROWS):
                if ok[kind][j][i]:
                    band = jnp.where(qrow == i, tt_scr[dr_idx[kind][j][i]], band)
            o_ref[0, kind, j * GRID_W:(j + 1) * GRID_W, :] = band


def _rotate_half_cols(w):
    w4 = w.reshape(w.shape[:-1] + (2, 2, ROPE_AXIS_FREQS))
    return jnp.stack([-w4[..., 1, :], w4[..., 0, :]], axis=-2).reshape(w.shape)


def _rope_tables(n):
    t = jnp.arange(n)
    pos = jnp.stack([t // GRID_W, t % GRID_W], axis=-1).astype(F32)
    inv = ROPE_THETA ** (-jnp.arange(ROPE_AXIS_FREQS, dtype=F32) / ROPE_AXIS_FREQS)
    ang = pos[:, :, None] * inv
    cos, sin = jnp.cos(ang), jnp.sin(ang)
    zeros = jnp.zeros((n, LANES - QK_ROPE_DIM), F32)
    cos2 = jnp.concatenate([cos[:, 0], cos[:, 0], cos[:, 1], cos[:, 1], zeros], axis=-1)
    sin2 = jnp.concatenate([sin[:, 0], sin[:, 0], sin[:, 1], sin[:, 1], zeros], axis=-1)
    return cos2.astype(F32), sin2.astype(F32)


def kernel(x_prompt, x_sample, cache_mla_ckv, cache_mla_krope, cache_na_k, cache_na_v, c, c_ctx,
           w_ada, b_ada, pre_norm_g, post_norm_g, mla_w_in, mla_q_norm_g, mla_w_qb, mla_kv_norm_g,
           mla_w_kvb, mla_w_out, na_w_in, na_rel_bias, na_w_out):
    bp, tp, d = x_prompt.shape
    bs, ts, _ = x_sample.shape
    tm = 512
    tmp = min(tm, tp)

    cond = jnp.zeros((MOD_ROWS, d), F32).at[:bs].set(c).at[CTX_ROW].set(c_ctx)
    mod = _modulation(cond, w_ada, b_ada)
    mod = mod.reshape(mod.shape[0], MOD_ROWS, 1, 3 * d)

    w_in = mla_w_in[0]
    n_small = Q_LORA_RANK + KV_LORA_RANK + QK_ROPE_DIM
    w_in_ext = jnp.concatenate(
        [w_in[:, :n_small], _rotate_half_cols(w_in[:, n_small - QK_ROPE_DIM:n_small]), w_in[:, n_small:]],
        axis=1).astype(BF16)
    wqb3 = mla_w_qb[0].reshape(Q_LORA_RANK, MLA_HEADS, QK_NOPE_DIM + QK_ROPE_DIM)
    wqb_ext = jnp.concatenate([wqb3, _rotate_half_cols(wqb3[..., QK_NOPE_DIM:])], axis=-1)
    wqb_ext = wqb_ext.reshape(Q_LORA_RANK, MLA_HEADS * 256).astype(BF16)
    wkvb3 = mla_w_kvb[0].reshape(KV_LORA_RANK, MLA_HEADS, QK_NOPE_DIM + V_HEAD_DIM)
    wkb = wkvb3[..., :QK_NOPE_DIM].transpose(1, 0, 2).astype(BF16)
    wvbt = wkvb3[..., QK_NOPE_DIM:].transpose(1, 2, 0).astype(BF16)
    preg0 = pre_norm_g[0].reshape(1, d)
    postg0 = post_norm_g[0].reshape(1, d)
    qg = mla_q_norm_g[0].reshape(1, Q_LORA_RANK)
    kvg = mla_kv_norm_g[0].reshape(1, KV_LORA_RANK)
    cos_s, sin_s = _rope_tables(ts)
    pad = jnp.zeros((tmp, LANES - QK_ROPE_DIM), F32)
    cos_p = jnp.concatenate([jnp.ones((tmp, QK_ROPE_DIM), F32), pad], axis=-1)
    sin_p = jnp.zeros((tmp, LANES), F32)
    w_out0 = mla_w_out[0].astype(BF16)

    qp, ckvb_p, krb_p, sg_p, ckv_p, kr_p = _mla_in(
        x_prompt, mod[0], CTX_ROW, preg0, w_in_ext, qg, wqb_ext, kvg, cos_p, sin_p, True, tmp)
    qs, ckvb_s, krb_s, sg_s = _mla_in(
        x_sample, mod[0], None, preg0, w_in_ext, qg, wqb_ext, kvg, cos_s, sin_s, False, tm)

    op = _mla_ctx_attn(qp, ckvb_p, krb_p, wkb, wvbt)
    osm = _mla_attn(qs, cache_mla_ckv, cache_mla_krope, 0, ckvb_s, krb_s, wkb, wvbt)

    xp = _out_proj(op, sg_p, x_prompt, mod[0], CTX_ROW, w_out0, postg0, tmp)
    xs = _out_proj(osm, sg_s, x_sample, mod[0], None, w_out0, postg0, tm)

    preg1 = pre_norm_g[1].reshape(1, d)
    postg1 = post_norm_g[1].reshape(1, d)
    na_win = na_w_in[0].astype(BF16)
    w_out1 = na_w_out[0].astype(BF16)
    q1p, k1p, vt1p, sg1p, k_state, v_state = _na_in(xp, mod[1], CTX_ROW, preg1, na_win, True, tmp)
    q1s, k1s, vt1s, sg1s = _na_in(xs, mod[1], None, preg1, na_win, False, tm)
    o1p = _na_ctx(q1p, k1p, vt1p)
    npast = cache_na_k.shape[2]
    kc = cache_na_k[:, 0].reshape(bs, npast, NA_WIDTH).astype(BF16)
    vct = cache_na_v[:, 0].reshape(bs, npast, NA_WIDTH).transpose(0, 2, 1).astype(BF16)
    bias = _na_bias_tables(na_rel_bias[0], ts // GRID_W)
    o1s = _na_lat(q1s, k1s, vt1s, kc, vct, bias)
    yp = _out_proj(o1p, sg1p, xp, mod[1], CTX_ROW, w_out1, postg1, tmp)
    ys = _out_proj(o1s, sg1s, xs, mod[1], None, w_out1, postg1, tm)

    return (yp, ys,
            ckv_p.reshape(bp, 1, tp, KV_LORA_RANK),
            kr_p.reshape(bp, 1, tp, QK_ROPE_DIM),
            k_state.reshape(bp, 1, tp, NA_HEADS, NA_HEAD_DIM),
            v_state.reshape(bp, 1, tp, NA_HEADS, NA_HEAD_DIM))
```

```python
import functools
import math

import numpy as np
import jax
import jax.numpy as jnp
from jax import lax
from jax.experimental import pallas as pl
from jax.experimental.pallas import tpu as pltpu

F32 = jnp.float32
BF16 = jnp.bfloat16

D_MODEL = 1024
GRID_W = 64
MLA_HEADS = 16
Q_LORA_RANK = 256
KV_LORA_RANK = 128
QK_NOPE_DIM = 128
QK_ROPE_DIM = 64
V_HEAD_DIM = 128
MLA_WIDTH = MLA_HEADS * V_HEAD_DIM
ROPE_AXIS_FREQS = QK_ROPE_DIM // 4
ROPE_THETA = 10000.0
NA_HEADS = 16
NA_HEAD_DIM = 64
NA_WIDTH = NA_HEADS * NA_HEAD_DIM
NA_MAX_ROWS = 8
NA_COLS = 16
EPS = 1e-6
LOG2E = math.log2(math.e)
MLA_QSCALE = (QK_NOPE_DIM + QK_ROPE_DIM) ** -0.5 * LOG2E
NA_QSCALE = NA_HEAD_DIM ** -0.5 * LOG2E

LANES = 128
SUBLANES = 8
BF16_ROWS = 16
MXU_DIM = 256
MOD_ROWS = 16
CTX_ROW = 8
VMEM_LIMIT = 48 * 1024 * 1024

NA_QROWS = 4
NA_WIN_ROWS = NA_QROWS + NA_MAX_ROWS

_NT = (((1,), (1,)), ((), ()))


def _params(sem, vmem=VMEM_LIMIT):
    return pltpu.CompilerParams(dimension_semantics=sem, vmem_limit_bytes=vmem)


def _rms(x, g):
    return x * lax.rsqrt(jnp.mean(x * x, axis=-1, keepdims=True) + EPS) * g


def _silu(x):
    return x * jax.nn.sigmoid(x)


def _mod_kernel(cond_ref, w_ref, b_ref, o_ref):
    o_ref[0] = jnp.dot(_silu(cond_ref[...]), w_ref[0], preferred_element_type=F32) + b_ref[0]


def _modulation(cond, w_ada, b_ada):
    depth, d, n = w_ada.shape
    bn = 768
    return pl.pallas_call(
        _mod_kernel,
        grid=(depth, n // bn),
        in_specs=[pl.BlockSpec((MOD_ROWS, d), lambda i, j: (0, 0)),
                  pl.BlockSpec((1, d, bn), lambda i, j: (i, 0, j)),
                  pl.BlockSpec((1, 1, bn), lambda i, j: (i, 0, j))],
        out_specs=pl.BlockSpec((1, MOD_ROWS, bn), lambda i, j: (i, 0, j)),
        out_shape=jax.ShapeDtypeStruct((depth, MOD_ROWS, n), F32),
        compiler_params=_params(("parallel", "parallel")),
        name="modulation",
    )(cond, w_ada, b_ada.reshape(depth, 1, n))


def _modulated(x, mod, g):
    return _rms(x, g) * (1.0 + mod[:, D_MODEL:2 * D_MODEL]) + mod[:, :D_MODEL]


def _mla_in_kernel(x_ref, mod_ref, preg_ref, win_ref, qg_ref, wqb_ref, kvg_ref, cos_ref, sin_ref,
                   q_ref, ckvb_ref, krb_ref, sg_ref, *state_refs):
    h = _modulated(x_ref[0], mod_ref[0], preg_ref[...]).astype(BF16)
    ya = jnp.dot(h, win_ref[:, :512], preferred_element_type=F32)
    ckv = _rms(ya[:, 256:384], kvg_ref[...])
    kr2 = ya[:, 384:512]
    cos2 = cos_ref[...]
    sin2 = sin_ref[...]
    ckvb_ref[0] = ckv.astype(BF16)
    krb_ref[0] = (kr2 * cos2 + pltpu.roll(kr2, 64, 1) * sin2).astype(BF16)
    if state_refs:
        state_refs[0][0] = ckv
        state_refs[1][0] = kr2[:, :QK_ROPE_DIM]
    qn = _rms(ya[:, :256], qg_ref[...]).astype(BF16)
    for hh in range(MLA_HEADS):
        yq = jnp.dot(qn, wqb_ref[:, hh * 256:(hh + 1) * 256], preferred_element_type=F32)
        sec = yq[:, 128:]
        q_ref[0, hh, :, 0:128] = (yq[:, :128] * MLA_QSCALE).astype(BF16)
        q_ref[0, hh, :, 128:256] = ((sec * cos2 + pltpu.roll(sec, 64, 1) * sin2) * MLA_QSCALE).astype(BF16)
    for c in range(MLA_WIDTH // 512):
        g = jnp.dot(h, win_ref[:, 512 + c * 512:1024 + c * 512], preferred_element_type=F32)
        sg_ref[0, :, c * 512:(c + 1) * 512] = _silu(g).astype(BF16)


def _mla_in(x, mod, mod_row, preg, win, qg, wqb, kvg, cos2, sin2, with_state, tm=512):
    b, t, d = x.shape
    nt = t // tm
    rope_blocks = cos2.shape[0] // tm
    tbl_idx = (lambda i, j: (j, 0)) if rope_blocks > 1 else (lambda i, j: (0, 0))
    mod_idx = (lambda i, j: (i, 0, 0)) if mod_row is None else (lambda i, j: (mod_row, 0, 0))
    const = lambda i, j: (0, 0)
    out_shape = [jax.ShapeDtypeStruct((b, MLA_HEADS, t, 256), BF16),
                 jax.ShapeDtypeStruct((b, t, 128), BF16),
                 jax.ShapeDtypeStruct((b, t, 128), BF16),
                 jax.ShapeDtypeStruct((b, t, MLA_WIDTH), BF16)]
    out_specs = [pl.BlockSpec((1, MLA_HEADS, tm, 256), lambda i, j: (i, 0, j, 0)),
                 pl.BlockSpec((1, tm, 128), lambda i, j: (i, j, 0)),
                 pl.BlockSpec((1, tm, 128), lambda i, j: (i, j, 0)),
                 pl.BlockSpec((1, tm, MLA_WIDTH), lambda i, j: (i, j, 0))]
    if with_state:
        out_shape += [jax.ShapeDtypeStruct((b, t, KV_LORA_RANK), F32),
                      jax.ShapeDtypeStruct((b, t, QK_ROPE_DIM), F32)]
        out_specs += [pl.BlockSpec((1, tm, KV_LORA_RANK), lambda i, j: (i, j, 0)),
                      pl.BlockSpec((1, tm, QK_ROPE_DIM), lambda i, j: (i, j, 0))]
    return pl.pallas_call(
        _mla_in_kernel,
        grid=(b, nt),
        in_specs=[pl.BlockSpec((1, tm, d), lambda i, j: (i, j, 0)),
                  pl.BlockSpec((1, 1, 3 * d), mod_idx),
                  pl.BlockSpec((1, d), const),
                  pl.BlockSpec(win.shape, const),
                  pl.BlockSpec((1, Q_LORA_RANK), const),
                  pl.BlockSpec(wqb.shape, const),
                  pl.BlockSpec((1, KV_LORA_RANK), const),
                  pl.BlockSpec((tm, 128), tbl_idx),
                  pl.BlockSpec((tm, 128), tbl_idx)],
        out_specs=out_specs,
        out_shape=out_shape,
        compiler_params=_params(("parallel", "parallel")),
        name="mla_in_proj",
    )(x, mod, preg, win, qg, wqb, kvg, cos2, sin2)


def _rows8(x, op):
    return op(x.reshape(x.shape[0] // SUBLANES, SUBLANES, x.shape[1]), axis=0)


def _pv_with_colsum(vts, ps):
    acc = None
    for vt, p in zip(vts, ps):
        lhs = jnp.concatenate([vt, jnp.ones((BF16_ROWS, vt.shape[1]), BF16)], axis=0)
        part = jnp.dot(lhs, p, preferred_element_type=F32)
        acc = part if acc is None else acc + part
    nd = vts[0].shape[0]
    return acc[:nd], acc[nd:nd + 1]


def _mla_attn_kernel(q_ref, cckv_ref, ckr_ref, ckv_ref, kr_ref, wkb_ref, wvbt_ref, sg_ref, o_ref,
                     k_scr, vt_scr, s_scr, *, tq, unroll):
    nctx = cckv_ref.shape[2]
    nk = nctx + ckv_ref.shape[1]
    chunks = [slice(lo, lo + MXU_DIM) for lo in range(0, nk, MXU_DIM)]
    for rows in chunks:
        if rows.start < nctx:
            c = cckv_ref[0, 0, rows, :].astype(BF16)
            k_scr[rows, 128:128 + QK_ROPE_DIM] = ckr_ref[0, 0, rows, :].astype(BF16)
            k_scr[rows, 128 + QK_ROPE_DIM:256] = jnp.zeros((MXU_DIM, 128 - QK_ROPE_DIM), BF16)
        else:
            lat = slice(rows.start - nctx, rows.stop - nctx)
            c = ckv_ref[0, lat, :]
            k_scr[rows, 128:256] = kr_ref[0, lat, :]
        k_scr[rows, 0:128] = jnp.dot(c, wkb_ref[0], preferred_element_type=F32).astype(BF16)
        vt_scr[:, rows] = lax.dot_general(wvbt_ref[0], c, _NT, preferred_element_type=F32).astype(BF16)

    def scores(qi, slot):
        q = q_ref[0, 0, pl.ds(pl.multiple_of(qi * tq, tq), tq), :]
        m8 = None
        for rows in chunks:
            st = lax.dot_general(k_scr[rows, :], q, _NT, preferred_element_type=F32)
            s_scr[slot, rows, :] = st
            cm = _rows8(st, jnp.max)
            m8 = cm if m8 is None else jnp.maximum(m8, cm)
        return jnp.max(m8, axis=0, keepdims=True)

    def finish(qi, slot, m):
        l8 = jnp.zeros((SUBLANES, tq), F32)
        acc = jnp.zeros((V_HEAD_DIM, tq), F32)
        for rows in chunks:
            p = jnp.exp2(s_scr[slot, rows, :] - m)
            l8 = l8 + _rows8(p, jnp.sum)
            acc = acc + jnp.dot(vt_scr[:, rows], p.astype(BF16), preferred_element_type=F32)
        l = jnp.sum(l8, axis=0, keepdims=True)
        rows = pl.ds(pl.multiple_of(qi * tq, tq), tq)
        o_ref[0, rows, :] = ((acc / l).T * sg_ref[0, rows, :].astype(F32)).astype(BF16)

    def run(q0, n, m):
        for i in range(n):
            m_next = scores(q0 + i + 1, (i + 1) % 2)
            finish(q0 + i, i % 2, m)
            m = m_next
        return m

    nq = q_ref.shape[2] // tq
    trips = nq // unroll - 1
    m = lax.fori_loop(0, trips, lambda j, m: run(unroll * j, unroll, m), scores(0, 0))
    m = run(trips * unroll, unroll - 1, m)
    finish(nq - 1, (unroll - 1) % 2, m)


def _mla_attn(q, cache_ckv, cache_kr, layer, ckv, kr, wkb, wvbt, sg, tq=256, unroll=4):
    b, nh, t, _ = q.shape
    nctx = cache_ckv.shape[2]
    nk = nctx + t
    assert nctx % MXU_DIM == 0 and t % (unroll * tq) == 0 and unroll % 2 == 0
    return pl.pallas_call(
        functools.partial(_mla_attn_kernel, tq=tq, unroll=unroll),
        grid=(b, nh),
        in_specs=[pl.BlockSpec((1, 1, t, 256), lambda i, h: (i, h, 0, 0)),
                  pl.BlockSpec((1, 1, nctx, KV_LORA_RANK), lambda i, h: (i, layer, 0, 0)),
                  pl.BlockSpec((1, 1, nctx, QK_ROPE_DIM), lambda i, h: (i, layer, 0, 0)),
                  pl.BlockSpec((1, t, 128), lambda i, h: (i, 0, 0)),
                  pl.BlockSpec((1, t, 128), lambda i, h: (i, 0, 0)),
                  pl.BlockSpec((1, 128, 128), lambda i, h: (h, 0, 0)),
                  pl.BlockSpec((1, 128, 128), lambda i, h: (h, 0, 0)),
                  pl.BlockSpec((1, t, V_HEAD_DIM), lambda i, h: (i, 0, h))],
        out_specs=pl.BlockSpec((1, t, V_HEAD_DIM), lambda i, h: (i, 0, h)),
        out_shape=jax.ShapeDtypeStruct((b, t, MLA_WIDTH), BF16),
        scratch_shapes=[pltpu.VMEM((nk, 256), BF16), pltpu.VMEM((V_HEAD_DIM, nk), BF16),
                        pltpu.VMEM((2, nk, tq), F32)],
        compiler_params=_params(("parallel", "parallel")),
        name="mla_attention",
    )(q, cache_ckv, cache_kr, ckv, kr, wkb, wvbt, sg)


def _mla_ctx_kernel(q_ref, ckv_ref, kr_ref, wkb_ref, wvbt_ref, sg_ref, o_ref):
    c = ckv_ref[0]
    kr = kr_ref[0]
    nh = q_ref.shape[1]

    def expand(h):
        k_nope = jnp.dot(c, wkb_ref[h], preferred_element_type=F32).astype(BF16)
        vt = lax.dot_general(wvbt_ref[h], c, _NT, preferred_element_type=F32).astype(BF16)
        return jnp.concatenate([k_nope, kr], axis=1), vt

    def scores(h, k):
        st = lax.dot_general(k, q_ref[0, h], _NT, preferred_element_type=F32)
        return st, jnp.max(_rows8(st, jnp.max), axis=0, keepdims=True)

    def finish(h, st, m, vt):
        p = jnp.exp2(st - m)
        l = jnp.sum(_rows8(p, jnp.sum), axis=0, keepdims=True)
        ot = jnp.dot(vt, p.astype(BF16), preferred_element_type=F32) / l
        cols = slice(h * V_HEAD_DIM, (h + 1) * V_HEAD_DIM)
        o_ref[0, :, cols] = (ot.T * sg_ref[0, :, cols].astype(F32)).astype(BF16)

    kv = {0: expand(0)}
    if nh > 1:
        kv[1] = expand(1)
    sm = {0: scores(0, kv[0][0])}
    for h in range(nh):
        if h + 2 < nh:
            kv[h + 2] = expand(h + 2)
        if h + 1 < nh:
            sm[h + 1] = scores(h + 1, kv[h + 1][0])
        finish(h, *sm.pop(h), kv.pop(h)[1])


def _mla_ctx_attn(q, ckv, kr, wkb, wvbt, sg):
    b, nh, t, _ = q.shape
    const3 = lambda i: (0, 0, 0)
    return pl.pallas_call(
        _mla_ctx_kernel,
        grid=(b,),
        in_specs=[pl.BlockSpec((1, nh, t, 256), lambda i: (i, 0, 0, 0)),
                  pl.BlockSpec((1, t, 128), lambda i: (i, 0, 0)),
                  pl.BlockSpec((1, t, 128), lambda i: (i, 0, 0)),
                  pl.BlockSpec(wkb.shape, const3),
                  pl.BlockSpec(wvbt.shape, const3),
                  pl.BlockSpec((1, t, MLA_WIDTH), lambda i: (i, 0, 0))],
        out_specs=pl.BlockSpec((1, t, MLA_WIDTH), lambda i: (i, 0, 0)),
        out_shape=jax.ShapeDtypeStruct((b, t, MLA_WIDTH), BF16),
        compiler_params=_params(("parallel",)),
        name="mla_ctx_attention",
    )(q, ckv, kr, wkb, wvbt, sg)


def _out_kernel(og_ref, x_ref, mod_ref, wout_ref, postg_ref, y_ref):
    out = jnp.dot(og_ref[0], wout_ref[...], preferred_element_type=F32)
    gate = mod_ref[0][:, 2 * D_MODEL:]
    y_ref[0] = x_ref[0] + gate * _rms(out, postg_ref[...])


def _out_proj(og, x, mod, mod_row, wout, postg, tm=512):
    b, t, d = x.shape
    w = og.shape[-1]
    mod_idx = (lambda i, j: (i, 0, 0)) if mod_row is None else (lambda i, j: (mod_row, 0, 0))
    const = lambda i, j: (0, 0)
    return pl.pallas_call(
        _out_kernel,
        grid=(b, t // tm),
        in_specs=[pl.BlockSpec((1, tm, w), lambda i, j: (i, j, 0)),
                  pl.BlockSpec((1, tm, d), lambda i, j: (i, j, 0)),
                  pl.BlockSpec((1, 1, 3 * d), mod_idx),
                  pl.BlockSpec(wout.shape, const),
                  pl.BlockSpec((1, d), const)],
        out_specs=pl.BlockSpec((1, tm, d), lambda i, j: (i, j, 0)),
        out_shape=jax.ShapeDtypeStruct((b, t, d), F32),
        compiler_params=_params(("parallel", "parallel")),
        name="out_proj",
    )(og, x, mod, wout, postg)


def _na_in_kernel(x_ref, mod_ref, preg_ref, win_ref, q_ref, k_ref, vt_ref, sg_ref, *state_refs):
    h = _modulated(x_ref[0], mod_ref[0], preg_ref[...]).astype(BF16)
    w = NA_WIDTH
    q = jnp.dot(h, win_ref[:, 0:w], preferred_element_type=F32)
    q_ref[0] = (q * NA_QSCALE).astype(BF16)
    k = jnp.dot(h, win_ref[:, w:2 * w], preferred_element_type=F32)
    k_ref[0] = k.astype(BF16)
    v = jnp.dot(h, win_ref[:, 2 * w:3 * w], preferred_element_type=F32)
    vt = v.T.astype(BF16)
    for ci in range(vt_ref.shape[1]):
        vt_ref[0, ci] = vt[:, ci * MXU_DIM:(ci + 1) * MXU_DIM]
    if state_refs:
        state_refs[0][0] = k
        state_refs[1][0] = v
    g = jnp.dot(h, win_ref[:, 3 * w:4 * w], preferred_element_type=F32)
    sg_ref[0] = _silu(g).astype(BF16)


def _na_in(x, mod, mod_row, preg, win, with_state, tm):
    b, t, d = x.shape
    w = NA_WIDTH
    mod_idx = (lambda i, j: (i, 0, 0)) if mod_row is None else (lambda i, j: (mod_row, 0, 0))
    const = lambda i, j: (0, 0)
    row_blk = pl.BlockSpec((1, tm, w), lambda i, j: (i, j, 0))
    out_shape = [jax.ShapeDtypeStruct((b, t, w), BF16), jax.ShapeDtypeStruct((b, t, w), BF16),
                 jax.ShapeDtypeStruct((b, t // MXU_DIM, w, MXU_DIM), BF16),
                 jax.ShapeDtypeStruct((b, t, w), BF16)]
    out_specs = [row_blk, row_blk,
                 pl.BlockSpec((1, tm // MXU_DIM, w, MXU_DIM), lambda i, j: (i, j, 0, 0)), row_blk]
    if with_state:
        out_shape += [jax.ShapeDtypeStruct((b, t, w), F32), jax.ShapeDtypeStruct((b, t, w), F32)]
        out_specs += [row_blk, row_blk]
    return pl.pallas_call(
        _na_in_kernel,
        grid=(b, t // tm),
        in_specs=[pl.BlockSpec((1, tm, d), lambda i, j: (i, j, 0)),
                  pl.BlockSpec((1, 1, 3 * d), mod_idx),
                  pl.BlockSpec((1, d), const),
                  pl.BlockSpec(win.shape, const)],
        out_specs=out_specs,
        out_shape=out_shape,
        compiler_params=_params(("parallel", "parallel")),
        name="na_in_proj",
    )(x, mod, preg, win)


def _head_masks(shape):
    lane = lax.broadcasted_iota(jnp.int32, shape, 1)
    return (lane < NA_HEAD_DIM, lane >= NA_HEAD_DIM)


def _na_ctx_kernel(q_ref, k_ref, vt_ref, sg_ref, o_ref):
    masks = _head_masks((q_ref.shape[1], LANES))
    nhead = q_ref.shape[2] // NA_HEAD_DIM

    def scores(h):
        cols = slice(h // 2 * LANES, (h // 2 + 1) * LANES)
        q2 = q_ref[0, :, cols]
        qm = jnp.where(masks[h % 2], q2, jnp.zeros_like(q2))
        st = lax.dot_general(k_ref[0, :, cols], qm, _NT, preferred_element_type=F32)
        return st, jnp.max(_rows8(st, jnp.max), axis=0, keepdims=True)

    def finish(h, st, m):
        p = jnp.exp2(st - m).astype(BF16)
        ot, l = _pv_with_colsum([vt_ref[0, 0, h * NA_HEAD_DIM:(h + 1) * NA_HEAD_DIM, :]], [p])
        return ot / l

    nxt = scores(0)
    outs = []
    for h in range(nhead):
        cur, nxt = nxt, (scores(h + 1) if h + 1 < nhead else None)
        outs.append(finish(h, *cur))
        if h % 2:
            cols = slice(h // 2 * LANES, (h // 2 + 1) * LANES)
            o2 = jnp.concatenate(outs, axis=0).T * sg_ref[0, :, cols].astype(F32)
            o_ref[0, :, cols] = o2.astype(BF16)
            outs = []


def _na_ctx(q, k, vt, sg):
    b, t, w = q.shape
    return pl.pallas_call(
        _na_ctx_kernel,
        grid=(b,),
        in_specs=[pl.BlockSpec((1, t, w), lambda i: (i, 0, 0)),
                  pl.BlockSpec((1, t, w), lambda i: (i, 0, 0)),
                  pl.BlockSpec((1, t // MXU_DIM, w, MXU_DIM), lambda i: (i, 0, 0, 0)),
                  pl.BlockSpec((1, t, w), lambda i: (i, 0, 0))],
        out_specs=pl.BlockSpec((1, t, w), lambda i: (i, 0, 0)),
        out_shape=jax.ShapeDtypeStruct((b, t, w), BF16),
        compiler_params=_params(("parallel",)),
        name="na_ctx_attention",
    )(q, k, vt, sg)


def _na_lat_kernel(q_ref, k_ref, vt_ref, kc_ref, vct_ref, bias_ref, sg_ref, o_ref,
                   s_scr, ot_scr, l_scr, *, nblk):
    tq = NA_QROWS * GRID_W
    nwin = NA_WIN_ROWS * GRID_W
    nctx = kc_ref.shape[1]
    masks = _head_masks((tq, LANES))

    def window(blk):
        ws = jnp.clip(blk * NA_QROWS - NA_MAX_ROWS // 2, 0, nblk * NA_QROWS - NA_WIN_ROWS)
        return pl.multiple_of(ws * GRID_W, MXU_DIM)

    def scores(blk, par):
        kind = jnp.where(blk == 0, 0, jnp.where(blk == nblk - 1, 2, 1))
        q2 = q_ref[0, pl.ds(pl.multiple_of(blk * tq, tq), tq), :]
        kwin = k_ref[0, pl.ds(window(blk), nwin), :]
        ms = []
        for hl in range(2):
            qm = jnp.where(masks[hl], q2, jnp.zeros_like(q2))
            s_loc = lax.dot_general(kwin, qm, _NT, preferred_element_type=F32) + bias_ref[hl, kind]
            s_ctx = lax.dot_general(kc_ref[0], qm, _NT, preferred_element_type=F32)
            s_scr[2 * par + hl, 0:nwin, :] = s_loc
            s_scr[2 * par + hl, nwin:nwin + nctx, :] = s_ctx
            m8 = jnp.maximum(_rows8(s_loc, jnp.max), _rows8(s_ctx, jnp.max))
            ms.append(jnp.max(m8, axis=0, keepdims=True))
        return tuple(ms)

    def finish(blk, par, ms):
        c0 = window(blk) // MXU_DIM
        for hl in range(2):
            rows = slice(hl * NA_HEAD_DIM, (hl + 1) * NA_HEAD_DIM)
            vts = [vt_ref[0, c0 + ci, rows, :] for ci in range(nwin // MXU_DIM)] + [vct_ref[0, rows, :]]
            edges = list(range(0, nwin + 1, MXU_DIM)) + [nwin + nctx]
            ps = [jnp.exp2(s_scr[2 * par + hl, lo:hi, :] - ms[hl]).astype(BF16)
                  for lo, hi in zip(edges[:-1], edges[1:])]
            ot, l = _pv_with_colsum(vts, ps)
            ot_scr[par, rows, :] = ot
            l_scr[par, hl:hl + 1, :] = l

    def writeout(blk, par):
        lrows = jnp.concatenate([jnp.broadcast_to(l_scr[par, hl:hl + 1, :], (NA_HEAD_DIM, tq))
                                 for hl in range(2)], axis=0)
        rows = pl.ds(pl.multiple_of(blk * tq, tq), tq)
        o_ref[0, rows, :] = ((ot_scr[par] / lrows).T * sg_ref[0, rows, :].astype(F32)).astype(BF16)

    def trip(j, ms):
        i = 2 * j + 1
        ms_even = scores(i + 1, 0)
        finish(i, 1, ms)
        writeout(i - 1, 0)
        ms_odd = scores(i + 2, 1)
        finish(i + 1, 0, ms_even)
        writeout(i, 1)
        return ms_odd

    ms0 = scores(0, 0)
    ms1 = scores(1, 1)
    finish(0, 0, ms0)
    ms_last = lax.fori_loop(0, nblk // 2 - 1, trip, ms1)
    finish(nblk - 1, 1, ms_last)
    writeout(nblk - 2, 0)
    writeout(nblk - 1, 1)


def _na_lat(q, k, vt, kc, vct, bias, sg):
    b, t, w = q.shape
    tq = NA_QROWS * GRID_W
    nblk = t // tq
    nctx = kc.shape[1]
    nwin = NA_WIN_ROWS * GRID_W
    return pl.pallas_call(
        functools.partial(_na_lat_kernel, nblk=nblk),
        grid=(w // LANES, b),
        in_specs=[pl.BlockSpec((1, t, LANES), lambda h, i: (i, 0, h)),
                  pl.BlockSpec((1, t, LANES), lambda h, i: (i, 0, h)),
                  pl.BlockSpec((1, t // MXU_DIM, LANES, MXU_DIM), lambda h, i: (i, 0, h, 0)),
                  pl.BlockSpec((1, nctx, LANES), lambda h, i: (i, 0, h)),
                  pl.BlockSpec((1, LANES, nctx), lambda h, i: (i, h, 0)),
                  pl.BlockSpec((2, 3, nwin, tq), lambda h, i: (h, 0, 0, 0)),
                  pl.BlockSpec((1, t, LANES), lambda h, i: (i, 0, h))],
        out_specs=pl.BlockSpec((1, t, LANES), lambda h, i: (i, 0, h)),
        out_shape=jax.ShapeDtypeStruct((b, t, w), BF16),
        scratch_shapes=[pltpu.VMEM((4, nwin + nctx, tq), F32), pltpu.VMEM((2, LANES, tq), F32),
                        pltpu.VMEM((2, SUBLANES, tq), F32)],
        compiler_params=_params(("parallel", "parallel")),
        name="na_lat_attention",
    )(q, k, vt, kc, vct, bias, sg)


def _na_bias_tables(rel_bias, rows):
    nblk = rows // NA_QROWS
    cols = np.arange(GRID_W)
    col_start = np.clip(cols - NA_COLS // 2, 0, GRID_W - NA_COLS)
    col_ok = (cols[:, None] >= col_start[None, :]) & (cols[:, None] < col_start[None, :] + NA_COLS)
    dc = np.clip(cols[:, None] - cols[None, :] + NA_COLS - 1, 0, 2 * NA_COLS - 2)
    dr = np.zeros((3, NA_WIN_ROWS, NA_QROWS), np.int32)
    ok = np.zeros((3, NA_WIN_ROWS, NA_QROWS), bool)
    for kind, blk in enumerate((0, 1, nblk - 1)):
        r0 = blk * NA_QROWS
        ws = int(np.clip(r0 - NA_MAX_ROWS // 2, 0, rows - NA_WIN_ROWS))
        for j in range(NA_WIN_ROWS):
            for i in range(NA_QROWS):
                rs = int(np.clip(r0 + i - NA_MAX_ROWS // 2, 0, rows - NA_MAX_ROWS))
                ok[kind, j, i] = rs <= ws + j < rs + NA_MAX_ROWS
                dr[kind, j, i] = np.clip(ws + j - (r0 + i) + NA_MAX_ROWS - 1, 0, 2 * NA_MAX_ROWS - 2)
    nh, n_dr, n_dc = rel_bias.shape
    half = LANES // GRID_W
    return pl.pallas_call(
        functools.partial(_na_bias_kernel, n_dr=n_dr, n_dc=n_dc, dr_idx=dr.tolist(), ok=ok.tolist()),
        grid=(nh,),
        in_specs=[pl.BlockSpec(memory_space=pltpu.SMEM),
                  pl.BlockSpec((GRID_W, LANES), lambda h: (0, 0)),
                  pl.BlockSpec((GRID_W, LANES), lambda h: (0, 0))],
        out_specs=pl.BlockSpec((1, 3, NA_WIN_ROWS * GRID_W, NA_QROWS * GRID_W), lambda h: (h, 0, 0, 0)),
        out_shape=jax.ShapeDtypeStruct((nh, 3, NA_WIN_ROWS * GRID_W, NA_QROWS * GRID_W), F32),
        scratch_shapes=[pltpu.VMEM((n_dr, GRID_W, NA_QROWS * GRID_W), F32)],
        compiler_params=_params(("parallel",)),
        name="na_bias_tables",
    )(rel_bias.reshape(-1).astype(F32),
      jnp.asarray(np.tile(dc, (1, half)), jnp.int32),
      jnp.asarray(np.tile(col_ok, (1, half)), jnp.int32))


def _na_bias_kernel(b_ref, dc_ref, colok_ref, o_ref, tt_scr, *, n_dr, n_dc, dr_idx, ok):
    base = pl.program_id(0) * (n_dr * n_dc)
    dcb = dc_ref[...]
    colok = colok_ref[...] > 0
    for d in range(n_dr):
        acc = jnp.zeros(dcb.shape, F32)
        for e in range(n_dc):
            acc = jnp.where(dcb == e, b_ref[base + d * n_dc + e], acc)
        tile = jnp.where(colok, acc * LOG2E, -jnp.inf)
        tt_scr[d] = jnp.concatenate([tile] * (tt_scr.shape[2] // LANES), axis=1)
    shape = tt_scr.shape[1:]
    qrow = lax.broadcasted_iota(jnp.int32, shape, 1) // GRID_W
    neg = jnp.full(shape, -jnp.inf, F32)
    for kind in range(3):
        for j in range(NA_WIN_ROWS):
            band = neg
            for i in range(NA_QROWS):
                if ok[kind][j][i]:
                    band = jnp.where(qrow == i, tt_scr[dr_idx[kind][j][i]], band)
            o_ref[0, kind, j * GRID_W:(j + 1) * GRID_W, :] = band


def _rotate_half_cols(w):
    w4 = w.reshape(w.shape[:-1] + (2, 2, ROPE_AXIS_FREQS))
    return jnp.stack([-w4[..., 1, :], w4[..., 0, :]], axis=-2).reshape(w.shape)


def _rope_tables(n):
    t = jnp.arange(n)
    pos = jnp.stack([t // GRID_W, t % GRID_W], axis=-1).astype(F32)
    inv = ROPE_THETA ** (-jnp.arange(ROPE_AXIS_FREQS, dtype=F32) / ROPE_AXIS_FREQS)
    ang = pos[:, :, None] * inv
    cos, sin = jnp.cos(ang), jnp.sin(ang)
    zeros = jnp.zeros((n, LANES - QK_ROPE_DIM), F32)
    cos2 = jnp.concatenate([cos[:, 0], cos[:, 0], cos[:, 1], cos[:, 1], zeros], axis=-1)
    sin2 = jnp.concatenate([sin[:, 0], sin[:, 0], sin[:, 1], sin[:, 1], zeros], axis=-1)
    return cos2.astype(F32), sin2.astype(F32)


def kernel(x_prompt, x_sample, cache_mla_ckv, cache_mla_krope, cache_na_k, cache_na_v, c, c_ctx,
           w_ada, b_ada, pre_norm_g, post_norm_g, mla_w_in, mla_q_norm_g, mla_w_qb, mla_kv_norm_g,
           mla_w_kvb, mla_w_out, na_w_in, na_rel_bias, na_w_out):
    bp, tp, d = x_prompt.shape
    bs, ts, _ = x_sample.shape
    tm = 512
    tmp = min(tm, tp)

    cond = jnp.zeros((MOD_ROWS, d), F32).at[:bs].set(c).at[CTX_ROW].set(c_ctx)
    mod = _modulation(cond, w_ada, b_ada)
    mod = mod.reshape(mod.shape[0], MOD_ROWS, 1, 3 * d)

    w_in = mla_w_in[0]
    n_small = Q_LORA_RANK + KV_LORA_RANK + QK_ROPE_DIM
    w_in_ext = jnp.concatenate(
        [w_in[:, :n_small], _rotate_half_cols(w_in[:, n_small - QK_ROPE_DIM:n_small]), w_in[:, n_small:]],
        axis=1).astype(BF16)
    wqb3 = mla_w_qb[0].reshape(Q_LORA_RANK, MLA_HEADS, QK_NOPE_DIM + QK_ROPE_DIM)
    wqb_ext = jnp.concatenate([wqb3, _rotate_half_cols(wqb3[..., QK_NOPE_DIM:])], axis=-1)
    wqb_ext = wqb_ext.reshape(Q_LORA_RANK, MLA_HEADS * 256).astype(BF16)
    wkvb3 = mla_w_kvb[0].reshape(KV_LORA_RANK, MLA_HEADS, QK_NOPE_DIM + V_HEAD_DIM)
    wkb = wkvb3[..., :QK_NOPE_DIM].transpose(1, 0, 2).astype(BF16)
    wvbt = wkvb3[..., QK_NOPE_DIM:].transpose(1, 2, 0).astype(BF16)
    preg0 = pre_norm_g[0].reshape(1, d)
    postg0 = post_norm_g[0].reshape(1, d)
    qg = mla_q_norm_g[0].reshape(1, Q_LORA_RANK)
    kvg = mla_kv_norm_g[0].reshape(1, KV_LORA_RANK)
    cos_s, sin_s = _rope_tables(ts)
    pad = jnp.zeros((tmp, LANES - QK_ROPE_DIM), F32)
    cos_p = jnp.concatenate([jnp.ones((tmp, QK_ROPE_DIM), F32), pad], axis=-1)
    sin_p = jnp.zeros((tmp, LANES), F32)
    w_out0 = mla_w_out[0].astype(BF16)

    qp, ckvb_p, krb_p, sg_p, ckv_p, kr_p = _mla_in(
        x_prompt, mod[0], CTX_ROW, preg0, w_in_ext, qg, wqb_ext, kvg, cos_p, sin_p, True, tmp)
    qs, ckvb_s, krb_s, sg_s = _mla_in(
        x_sample, mod[0], None, preg0, w_in_ext, qg, wqb_ext, kvg, cos_s, sin_s, False, tm)

    op = _mla_ctx_attn(qp, ckvb_p, krb_p, wkb, wvbt, sg_p)
    osm = _mla_attn(qs, cache_mla_ckv, cache_mla_krope, 0, ckvb_s, krb_s, wkb, wvbt, sg_s)

    xp = _out_proj(op, x_prompt, mod[0], CTX_ROW, w_out0, postg0, tmp)
    xs = _out_proj(osm, x_sample, mod[0], None, w_out0, postg0, tm)

    preg1 = pre_norm_g[1].reshape(1, d)
    postg1 = post_norm_g[1].reshape(1, d)
    na_win = na_w_in[0].astype(BF16)
    w_out1 = na_w_out[0].astype(BF16)
    q1p, k1p, vt1p, sg1p, k_state, v_state = _na_in(xp, mod[1], CTX_ROW, preg1, na_win, True, tmp)
    q1s, k1s, vt1s, sg1s = _na_in(xs, mod[1], None, preg1, na_win, False, tm)
    o1p = _na_ctx(q1p, k1p, vt1p, sg1p)
    npast = cache_na_k.shape[2]
    kc = cache_na_k[:, 0].reshape(bs, npast, NA_WIDTH).astype(BF16)
    vct = cache_na_v[:, 0].reshape(bs, npast, NA_WIDTH).transpose(0, 2, 1).astype(BF16)
    bias = _na_bias_tables(na_rel_bias[0], ts // GRID_W)
    o1s = _na_lat(q1s, k1s, vt1s, kc, vct, bias, sg1s)
    yp = _out_proj(o1p, xp, mod[1], CTX_ROW, w_out1, postg1, tmp)
    ys = _out_proj(o1s, xs, mod[1], None, w_out1, postg1, tm)

    return (yp, ys,
            ckv_p.reshape(bp, 1, tp, KV_LORA_RANK),
            kr_p.reshape(bp, 1, tp, QK_ROPE_DIM),
            k_state.reshape(bp, 1, tp, NA_HEADS, NA_HEAD_DIM),
            v_state.reshape(bp, 1, tp, NA_HEADS, NA_HEAD_DIM))
```

```python
import functools
import math

import numpy as np
import jax
import jax.numpy as jnp
from jax import lax
from jax.experimental import pallas as pl
from jax.experimental.pallas import tpu as pltpu

F32 = jnp.float32
BF16 = jnp.bfloat16

D_MODEL = 1024
GRID_W = 64
MLA_HEADS = 16
Q_LORA_RANK = 256
KV_LORA_RANK = 128
QK_NOPE_DIM = 128
QK_ROPE_DIM = 64
V_HEAD_DIM = 128
MLA_WIDTH = MLA_HEADS * V_HEAD_DIM
ROPE_AXIS_FREQS = QK_ROPE_DIM // 4
ROPE_THETA = 10000.0
NA_HEADS = 16
NA_HEAD_DIM = 64
NA_WIDTH = NA_HEADS * NA_HEAD_DIM
NA_MAX_ROWS = 8
NA_COLS = 16
EPS = 1e-6
LOG2E = math.log2(math.e)
MLA_QSCALE = (QK_NOPE_DIM + QK_ROPE_DIM) ** -0.5 * LOG2E
NA_QSCALE = NA_HEAD_DIM ** -0.5 * LOG2E

LANES = 128
SUBLANES = 8
BF16_ROWS = 16
CTX_GROUP = 4
MXU_DIM = 256
MOD_ROWS = 16
CTX_ROW = 8
VMEM_LIMIT = 48 * 1024 * 1024

NA_QROWS = 4
NA_WIN_ROWS = NA_QROWS + NA_MAX_ROWS

_NT = (((1,), (1,)), ((), ()))


def _params(sem, vmem=VMEM_LIMIT):
    return pltpu.CompilerParams(dimension_semantics=sem, vmem_limit_bytes=vmem)


def _rms(x, g):
    return x * lax.rsqrt(jnp.mean(x * x, axis=-1, keepdims=True) + EPS) * g


def _silu(x):
    return x * jax.nn.sigmoid(x)


def _mod_kernel(cond_ref, w_ref, b_ref, o_ref):
    o_ref[0] = jnp.dot(_silu(cond_ref[...]), w_ref[0], preferred_element_type=F32) + b_ref[0]


def _modulation(cond, w_ada, b_ada):
    depth, d, n = w_ada.shape
    bn = 768
    return pl.pallas_call(
        _mod_kernel,
        grid=(depth, n // bn),
        in_specs=[pl.BlockSpec((MOD_ROWS, d), lambda i, j: (0, 0)),
                  pl.BlockSpec((1, d, bn), lambda i, j: (i, 0, j)),
                  pl.BlockSpec((1, 1, bn), lambda i, j: (i, 0, j))],
        out_specs=pl.BlockSpec((1, MOD_ROWS, bn), lambda i, j: (i, 0, j)),
        out_shape=jax.ShapeDtypeStruct((depth, MOD_ROWS, n), F32),
        compiler_params=_params(("parallel", "parallel")),
        name="modulation",
    )(cond, w_ada, b_ada.reshape(depth, 1, n))


def _modulated(x, mod, g):
    return _rms(x, g) * (1.0 + mod[:, D_MODEL:2 * D_MODEL]) + mod[:, :D_MODEL]


def _mla_in_kernel(x_ref, mod_ref, preg_ref, win_ref, qg_ref, wqb_ref, kvg_ref, cos_ref, sin_ref,
                   q_ref, ckvb_ref, krb_ref, sg_ref, *state_refs):
    h = _modulated(x_ref[0], mod_ref[0], preg_ref[...]).astype(BF16)
    ya = jnp.dot(h, win_ref[:, :512], preferred_element_type=F32)
    ckv = _rms(ya[:, 256:384], kvg_ref[...])
    kr2 = ya[:, 384:512]
    cos2 = cos_ref[...]
    sin2 = sin_ref[...]
    ckvb_ref[0] = ckv.astype(BF16)
    krb_ref[0] = (kr2 * cos2 + pltpu.roll(kr2, 64, 1) * sin2).astype(BF16)
    if state_refs:
        state_refs[0][0] = ckv
        state_refs[1][0] = kr2[:, :QK_ROPE_DIM]
    qn = _rms(ya[:, :256], qg_ref[...]).astype(BF16)
    for hh in range(MLA_HEADS):
        yq = jnp.dot(qn, wqb_ref[:, hh * 256:(hh + 1) * 256], preferred_element_type=F32)
        sec = yq[:, 128:]
        q_ref[0, hh, :, 0:128] = (yq[:, :128] * MLA_QSCALE).astype(BF16)
        q_ref[0, hh, :, 128:256] = ((sec * cos2 + pltpu.roll(sec, 64, 1) * sin2) * MLA_QSCALE).astype(BF16)
    for c in range(MLA_WIDTH // 512):
        g = jnp.dot(h, win_ref[:, 512 + c * 512:1024 + c * 512], preferred_element_type=F32)
        sg_ref[0, :, c * 512:(c + 1) * 512] = _silu(g).astype(BF16)


def _mla_in(x, mod, mod_row, preg, win, qg, wqb, kvg, cos2, sin2, with_state, tm=512):
    b, t, d = x.shape
    nt = t // tm
    rope_blocks = cos2.shape[0] // tm
    tbl_idx = (lambda i, j: (j, 0)) if rope_blocks > 1 else (lambda i, j: (0, 0))
    mod_idx = (lambda i, j: (i, 0, 0)) if mod_row is None else (lambda i, j: (mod_row, 0, 0))
    const = lambda i, j: (0, 0)
    out_shape = [jax.ShapeDtypeStruct((b, MLA_HEADS, t, 256), BF16),
                 jax.ShapeDtypeStruct((b, t, 128), BF16),
                 jax.ShapeDtypeStruct((b, t, 128), BF16),
                 jax.ShapeDtypeStruct((b, t, MLA_WIDTH), BF16)]
    out_specs = [pl.BlockSpec((1, MLA_HEADS, tm, 256), lambda i, j: (i, 0, j, 0)),
                 pl.BlockSpec((1, tm, 128), lambda i, j: (i, j, 0)),
                 pl.BlockSpec((1, tm, 128), lambda i, j: (i, j, 0)),
                 pl.BlockSpec((1, tm, MLA_WIDTH), lambda i, j: (i, j, 0))]
    if with_state:
        out_shape += [jax.ShapeDtypeStruct((b, t, KV_LORA_RANK), F32),
                      jax.ShapeDtypeStruct((b, t, QK_ROPE_DIM), F32)]
        out_specs += [pl.BlockSpec((1, tm, KV_LORA_RANK), lambda i, j: (i, j, 0)),
                      pl.BlockSpec((1, tm, QK_ROPE_DIM), lambda i, j: (i, j, 0))]
    return pl.pallas_call(
        _mla_in_kernel,
        grid=(b, nt),
        in_specs=[pl.BlockSpec((1, tm, d), lambda i, j: (i, j, 0)),
                  pl.BlockSpec((1, 1, 3 * d), mod_idx),
                  pl.BlockSpec((1, d), const),
                  pl.BlockSpec(win.shape, const),
                  pl.BlockSpec((1, Q_LORA_RANK), const),
                  pl.BlockSpec(wqb.shape, const),
                  pl.BlockSpec((1, KV_LORA_RANK), const),
                  pl.BlockSpec((tm, 128), tbl_idx),
                  pl.BlockSpec((tm, 128), tbl_idx)],
        out_specs=out_specs,
        out_shape=out_shape,
        compiler_params=_params(("parallel", "parallel")),
        name="mla_in_proj",
    )(x, mod, preg, win, qg, wqb, kvg, cos2, sin2)


def _rows8(x, op):
    return op(x.reshape(x.shape[0] // SUBLANES, SUBLANES, x.shape[1]), axis=0)


def _pv_with_colsum(vts, ps):
    acc = None
    for vt, p in zip(vts, ps):
        lhs = jnp.concatenate([vt, jnp.ones((BF16_ROWS, vt.shape[1]), BF16)], axis=0)
        part = jnp.dot(lhs, p, preferred_element_type=F32)
        acc = part if acc is None else acc + part
    nd = vts[0].shape[0]
    return acc[:nd], acc[nd:nd + 1]


def _mla_attn_kernel(q_ref, cckv_ref, ckr_ref, ckv_ref, kr_ref, wkb_ref, wvbt_ref, sg_ref, o_ref,
                     k_scr, vt_scr, s_scr, acc_scr, l_scr, *, tq, unroll):
    nctx = cckv_ref.shape[2]
    nk = nctx + ckv_ref.shape[1]
    chunks = [slice(lo, lo + MXU_DIM) for lo in range(0, nk, MXU_DIM)]
    for rows in chunks:
        if rows.start < nctx:
            c = cckv_ref[0, 0, rows, :].astype(BF16)
            k_scr[rows, 128:128 + QK_ROPE_DIM] = ckr_ref[0, 0, rows, :].astype(BF16)
            k_scr[rows, 128 + QK_ROPE_DIM:256] = jnp.zeros((MXU_DIM, 128 - QK_ROPE_DIM), BF16)
        else:
            lat = slice(rows.start - nctx, rows.stop - nctx)
            c = ckv_ref[0, lat, :]
            k_scr[rows, 128:256] = kr_ref[0, lat, :]
        k_scr[rows, 0:128] = jnp.dot(c, wkb_ref[0], preferred_element_type=F32).astype(BF16)
        vt_scr[:, rows] = lax.dot_general(wvbt_ref[0], c, _NT, preferred_element_type=F32).astype(BF16)

    def scores(qi, slot):
        q = q_ref[0, 0, pl.ds(pl.multiple_of(qi * tq, tq), tq), :]
        m8 = None
        for rows in chunks:
            st = lax.dot_general(k_scr[rows, :], q, _NT, preferred_element_type=F32)
            s_scr[slot, rows, :] = st
            cm = _rows8(st, jnp.max)
            m8 = cm if m8 is None else jnp.maximum(m8, cm)
        return jnp.max(m8, axis=0, keepdims=True)

    def finish(qi, slot, m):
        l8 = jnp.zeros((SUBLANES, tq), F32)
        acc = jnp.zeros((V_HEAD_DIM, tq), F32)
        for rows in chunks:
            p = jnp.exp2(s_scr[slot, rows, :] - m)
            l8 = l8 + _rows8(p, jnp.sum)
            acc = acc + jnp.dot(vt_scr[:, rows], p.astype(BF16), preferred_element_type=F32)
        acc_scr[slot] = acc
        l_scr[slot] = l8

    def writeout(qi, slot):
        l = jnp.sum(l_scr[slot], axis=0, keepdims=True)
        rows = pl.ds(pl.multiple_of(qi * tq, tq), tq)
        o_ref[0, rows, :] = ((acc_scr[slot] / l).T * sg_ref[0, rows, :].astype(F32)).astype(BF16)

    def stages(first, n, parity, m):
        for t in range(n):
            par = (parity + t) % 2
            m_next = scores(first + t + 1, 1 - par)
            finish(first + t, par, m)
            writeout(first + t - 1, 1 - par)
            m = m_next
        return m

    nq = q_ref.shape[2] // tq
    m0 = scores(0, 0)
    m = scores(1, 1)
    finish(0, 0, m0)
    trips, rest = divmod(nq - 2, unroll)
    m = lax.fori_loop(0, trips, lambda j, m: stages(unroll * j + 1, unroll, 1, m), m)
    m = stages(trips * unroll + 1, rest, 1, m)
    finish(nq - 1, (nq - 1) % 2, m)
    writeout(nq - 2, nq % 2)
    writeout(nq - 1, (nq - 1) % 2)


def _mla_attn(q, cache_ckv, cache_kr, layer, ckv, kr, wkb, wvbt, sg, tq=256, unroll=4):
    b, nh, t, _ = q.shape
    nctx = cache_ckv.shape[2]
    nk = nctx + t
    assert nctx % MXU_DIM == 0 and t % (unroll * tq) == 0 and unroll % 2 == 0
    return pl.pallas_call(
        functools.partial(_mla_attn_kernel, tq=tq, unroll=unroll),
        grid=(b, nh),
        in_specs=[pl.BlockSpec((1, 1, t, 256), lambda i, h: (i, h, 0, 0)),
                  pl.BlockSpec((1, 1, nctx, KV_LORA_RANK), lambda i, h: (i, layer, 0, 0)),
                  pl.BlockSpec((1, 1, nctx, QK_ROPE_DIM), lambda i, h: (i, layer, 0, 0)),
                  pl.BlockSpec((1, t, 128), lambda i, h: (i, 0, 0)),
                  pl.BlockSpec((1, t, 128), lambda i, h: (i, 0, 0)),
                  pl.BlockSpec((1, 128, 128), lambda i, h: (h, 0, 0)),
                  pl.BlockSpec((1, 128, 128), lambda i, h: (h, 0, 0)),
                  pl.BlockSpec((1, t, V_HEAD_DIM), lambda i, h: (i, 0, h))],
        out_specs=pl.BlockSpec((1, t, V_HEAD_DIM), lambda i, h: (i, 0, h)),
        out_shape=jax.ShapeDtypeStruct((b, t, MLA_WIDTH), BF16),
        scratch_shapes=[pltpu.VMEM((nk, 256), BF16), pltpu.VMEM((V_HEAD_DIM, nk), BF16),
                        pltpu.VMEM((2, nk, tq), F32), pltpu.VMEM((2, V_HEAD_DIM, tq), F32),
                        pltpu.VMEM((2, SUBLANES, tq), F32)],
        compiler_params=_params(("parallel", "parallel")),
        name="mla_attention",
    )(q, cache_ckv, cache_kr, ckv, kr, wkb, wvbt, sg)


def _mla_ctx_kernel(q_ref, ckv_ref, kr_ref, wkb_ref, wvbt_ref, sg_ref, o_ref):
    c = ckv_ref[0]
    kr = kr_ref[0]
    nh = q_ref.shape[1]

    def expand(h):
        k_nope = jnp.dot(c, wkb_ref[h], preferred_element_type=F32).astype(BF16)
        vt = lax.dot_general(wvbt_ref[h], c, _NT, preferred_element_type=F32).astype(BF16)
        return jnp.concatenate([k_nope, kr], axis=1), vt

    def scores(h, k):
        st = lax.dot_general(k, q_ref[0, h], _NT, preferred_element_type=F32)
        return st, jnp.max(_rows8(st, jnp.max), axis=0, keepdims=True)

    def finish(h, st, m, vt):
        p = jnp.exp2(st - m)
        l = jnp.sum(_rows8(p, jnp.sum), axis=0, keepdims=True)
        ot = jnp.dot(vt, p.astype(BF16), preferred_element_type=F32) / l
        cols = slice(h * V_HEAD_DIM, (h + 1) * V_HEAD_DIM)
        o_ref[0, :, cols] = (ot.T * sg_ref[0, :, cols].astype(F32)).astype(BF16)

    groups = [range(g, min(g + CTX_GROUP, nh)) for g in range(0, nh, CTX_GROUP)]
    ng = len(groups)
    kv = {g: [expand(h) for h in groups[g]] for g in range(min(2, ng))}
    sm = {0: [scores(h, k) for h, (k, _) in zip(groups[0], kv[0])]}
    for g in range(ng):
        if g + 2 < ng:
            kv[g + 2] = [expand(h) for h in groups[g + 2]]
        if g + 1 < ng:
            sm[g + 1] = [scores(h, k) for h, (k, _) in zip(groups[g + 1], kv[g + 1])]
        for h, (st, m), (_, vt) in zip(groups[g], sm.pop(g), kv.pop(g)):
            finish(h, st, m, vt)


def _mla_ctx_attn(q, ckv, kr, wkb, wvbt, sg):
    b, nh, t, _ = q.shape
    const3 = lambda i: (0, 0, 0)
    return pl.pallas_call(
        _mla_ctx_kernel,
        grid=(b,),
        in_specs=[pl.BlockSpec((1, nh, t, 256), lambda i: (i, 0, 0, 0)),
                  pl.BlockSpec((1, t, 128), lambda i: (i, 0, 0)),
                  pl.BlockSpec((1, t, 128), lambda i: (i, 0, 0)),
                  pl.BlockSpec(wkb.shape, const3),
                  pl.BlockSpec(wvbt.shape, const3),
                  pl.BlockSpec((1, t, MLA_WIDTH), lambda i: (i, 0, 0))],
        out_specs=pl.BlockSpec((1, t, MLA_WIDTH), lambda i: (i, 0, 0)),
        out_shape=jax.ShapeDtypeStruct((b, t, MLA_WIDTH), BF16),
        compiler_params=_params(("parallel",)),
        name="mla_ctx_attention",
    )(q, ckv, kr, wkb, wvbt, sg)


def _out_kernel(og_ref, x_ref, mod_ref, wout_ref, postg_ref, y_ref):
    out = jnp.dot(og_ref[0], wout_ref[...], preferred_element_type=F32)
    gate = mod_ref[0][:, 2 * D_MODEL:]
    y_ref[0] = x_ref[0] + gate * _rms(out, postg_ref[...])


def _out_proj(og, x, mod, mod_row, wout, postg, tm=512):
    b, t, d = x.shape
    w = og.shape[-1]
    mod_idx = (lambda i, j: (i, 0, 0)) if mod_row is None else (lambda i, j: (mod_row, 0, 0))
    const = lambda i, j: (0, 0)
    return pl.pallas_call(
        _out_kernel,
        grid=(b, t // tm),
        in_specs=[pl.BlockSpec((1, tm, w), lambda i, j: (i, j, 0)),
                  pl.BlockSpec((1, tm, d), lambda i, j: (i, j, 0)),
                  pl.BlockSpec((1, 1, 3 * d), mod_idx),
                  pl.BlockSpec(wout.shape, const),
                  pl.BlockSpec((1, d), const)],
        out_specs=pl.BlockSpec((1, tm, d), lambda i, j: (i, j, 0)),
        out_shape=jax.ShapeDtypeStruct((b, t, d), F32),
        compiler_params=_params(("parallel", "parallel")),
        name="out_proj",
    )(og, x, mod, wout, postg)


def _na_in_kernel(x_ref, mod_ref, preg_ref, win_ref, q_ref, k_ref, vt_ref, sg_ref, *state_refs):
    h = _modulated(x_ref[0], mod_ref[0], preg_ref[...]).astype(BF16)
    w = NA_WIDTH
    q = jnp.dot(h, win_ref[:, 0:w], preferred_element_type=F32)
    q_ref[0] = (q * NA_QSCALE).astype(BF16)
    k = jnp.dot(h, win_ref[:, w:2 * w], preferred_element_type=F32)
    k_ref[0] = k.astype(BF16)
    v = jnp.dot(h, win_ref[:, 2 * w:3 * w], preferred_element_type=F32)
    vt = v.T.astype(BF16)
    for ci in range(vt_ref.shape[1]):
        vt_ref[0, ci] = vt[:, ci * MXU_DIM:(ci + 1) * MXU_DIM]
    if state_refs:
        state_refs[0][0] = k
        state_refs[1][0] = v
    g = jnp.dot(h, win_ref[:, 3 * w:4 * w], preferred_element_type=F32)
    sg_ref[0] = _silu(g).astype(BF16)


def _na_in(x, mod, mod_row, preg, win, with_state, tm):
    b, t, d = x.shape
    w = NA_WIDTH
    mod_idx = (lambda i, j: (i, 0, 0)) if mod_row is None else (lambda i, j: (mod_row, 0, 0))
    const = lambda i, j: (0, 0)
    row_blk = pl.BlockSpec((1, tm, w), lambda i, j: (i, j, 0))
    out_shape = [jax.ShapeDtypeStruct((b, t, w), BF16), jax.ShapeDtypeStruct((b, t, w), BF16),
                 jax.ShapeDtypeStruct((b, t // MXU_DIM, w, MXU_DIM), BF16),
                 jax.ShapeDtypeStruct((b, t, w), BF16)]
    out_specs = [row_blk, row_blk,
                 pl.BlockSpec((1, tm // MXU_DIM, w, MXU_DIM), lambda i, j: (i, j, 0, 0)), row_blk]
    if with_state:
        out_shape += [jax.ShapeDtypeStruct((b, t, w), F32), jax.ShapeDtypeStruct((b, t, w), F32)]
        out_specs += [row_blk, row_blk]
    return pl.pallas_call(
        _na_in_kernel,
        grid=(b, t // tm),
        in_specs=[pl.BlockSpec((1, tm, d), lambda i, j: (i, j, 0)),
                  pl.BlockSpec((1, 1, 3 * d), mod_idx),
                  pl.BlockSpec((1, d), const),
                  pl.BlockSpec(win.shape, const)],
        out_specs=out_specs,
        out_shape=out_shape,
        compiler_params=_params(("parallel", "parallel")),
        name="na_in_proj",
    )(x, mod, preg, win)


def _head_masks(shape):
    lane = lax.broadcasted_iota(jnp.int32, shape, 1)
    return (lane < NA_HEAD_DIM, lane >= NA_HEAD_DIM)


def _na_ctx_kernel(q_ref, k_ref, vt_ref, sg_ref, o_ref):
    masks = _head_masks((q_ref.shape[1], LANES))
    nhead = q_ref.shape[2] // NA_HEAD_DIM

    def scores(h):
        cols = slice(h // 2 * LANES, (h // 2 + 1) * LANES)
        q2 = q_ref[0, :, cols]
        qm = jnp.where(masks[h % 2], q2, jnp.zeros_like(q2))
        st = lax.dot_general(k_ref[0, :, cols], qm, _NT, preferred_element_type=F32)
        return st, jnp.max(_rows8(st, jnp.max), axis=0, keepdims=True)

    def finish(h, st, m):
        p = jnp.exp2(st - m).astype(BF16)
        ot, l = _pv_with_colsum([vt_ref[0, 0, h * NA_HEAD_DIM:(h + 1) * NA_HEAD_DIM, :]], [p])
        return ot / l

    def store(hp, outs):
        cols = slice(hp * LANES, (hp + 1) * LANES)
        o2 = jnp.concatenate(outs, axis=0).T * sg_ref[0, :, cols].astype(F32)
        o_ref[0, :, cols] = o2.astype(BF16)

    groups = [range(g, g + CTX_GROUP) for g in range(0, nhead, CTX_GROUP)]
    nxt = [scores(h) for h in groups[0]]
    for gi, group in enumerate(groups):
        cur, nxt = nxt, ([scores(h) for h in groups[gi + 1]] if gi + 1 < len(groups) else None)
        outs = [finish(h, *sm) for h, sm in zip(group, cur)]
        for hp in range(group[0] // 2, group[-1] // 2 + 1):
            store(hp, outs[2 * hp - group[0]:2 * hp - group[0] + 2])


def _na_ctx(q, k, vt, sg):
    b, t, w = q.shape
    return pl.pallas_call(
        _na_ctx_kernel,
        grid=(b,),
        in_specs=[pl.BlockSpec((1, t, w), lambda i: (i, 0, 0)),
                  pl.BlockSpec((1, t, w), lambda i: (i, 0, 0)),
                  pl.BlockSpec((1, t // MXU_DIM, w, MXU_DIM), lambda i: (i, 0, 0, 0)),
                  pl.BlockSpec((1, t, w), lambda i: (i, 0, 0))],
        out_specs=pl.BlockSpec((1, t, w), lambda i: (i, 0, 0)),
        out_shape=jax.ShapeDtypeStruct((b, t, w), BF16),
        compiler_params=_params(("parallel",)),
        name="na_ctx_attention",
    )(q, k, vt, sg)


def _na_lat_kernel(q_ref, k_ref, vt_ref, kc_ref, vct_ref, bias_ref, sg_ref, o_ref,
                   s_scr, ot_scr, l_scr, *, nblk):
    tq = NA_QROWS * GRID_W
    nwin = NA_WIN_ROWS * GRID_W
    nctx = kc_ref.shape[1]
    masks = _head_masks((tq, LANES))

    def window(blk):
        ws = jnp.clip(blk * NA_QROWS - NA_MAX_ROWS // 2, 0, nblk * NA_QROWS - NA_WIN_ROWS)
        return pl.multiple_of(ws * GRID_W, MXU_DIM)

    def scores(blk, par, heads=(0, 1)):
        kind = jnp.where(blk == 0, 0, jnp.where(blk == nblk - 1, 2, 1))
        q2 = q_ref[0, pl.ds(pl.multiple_of(blk * tq, tq), tq), :]
        kwin = k_ref[0, pl.ds(window(blk), nwin), :]
        ms = []
        for hl in heads:
            qm = jnp.where(masks[hl], q2, jnp.zeros_like(q2))
            s_loc = lax.dot_general(kwin, qm, _NT, preferred_element_type=F32) + bias_ref[hl, kind]
            s_ctx = lax.dot_general(kc_ref[0], qm, _NT, preferred_element_type=F32)
            s_scr[2 * par + hl, 0:nwin, :] = s_loc
            s_scr[2 * par + hl, nwin:nwin + nctx, :] = s_ctx
            m8 = jnp.maximum(_rows8(s_loc, jnp.max), _rows8(s_ctx, jnp.max))
            ms.append(jnp.max(m8, axis=0, keepdims=True))
        return tuple(ms)

    def finish(blk, par, ms, heads=(0, 1)):
        c0 = window(blk) // MXU_DIM
        for hl in heads:
            rows = slice(hl * NA_HEAD_DIM, (hl + 1) * NA_HEAD_DIM)
            vts = [vt_ref[0, c0 + ci, rows, :] for ci in range(nwin // MXU_DIM)] + [vct_ref[0, rows, :]]
            edges = list(range(0, nwin + 1, MXU_DIM)) + [nwin + nctx]
            ps = [jnp.exp2(s_scr[2 * par + hl, lo:hi, :] - ms[hl]).astype(BF16)
                  for lo, hi in zip(edges[:-1], edges[1:])]
            ot, l = _pv_with_colsum(vts, ps)
            ot_scr[par, rows, :] = ot
            l_scr[par, hl:hl + 1, :] = l

    def writeout(blk, par):
        lrows = jnp.concatenate([jnp.broadcast_to(l_scr[par, hl:hl + 1, :], (NA_HEAD_DIM, tq))
                                 for hl in range(2)], axis=0)
        rows = pl.ds(pl.multiple_of(blk * tq, tq), tq)
        o_ref[0, rows, :] = ((ot_scr[par] / lrows).T * sg_ref[0, rows, :].astype(F32)).astype(BF16)

    def stage(i, par, ms):
        ms_next = ()
        for hl in range(2):
            ms_next += scores(i + 1, 1 - par, (hl,))
            finish(i, par, ms, (hl,))
            if hl == 0:
                writeout(i - 1, 1 - par)
        return ms_next

    ms0 = scores(0, 0)
    ms1 = scores(1, 1)
    finish(0, 0, ms0)
    ms_last = lax.fori_loop(0, nblk // 2 - 1, lambda j, ms: stage(2 * j + 2, 0, stage(2 * j + 1, 1, ms)), ms1)
    finish(nblk - 1, 1, ms_last)
    writeout(nblk - 2, 0)
    writeout(nblk - 1, 1)


def _na_lat(q, k, vt, kc, vct, bias, sg):
    b, t, w = q.shape
    tq = NA_QROWS * GRID_W
    nblk = t // tq
    nctx = kc.shape[1]
    nwin = NA_WIN_ROWS * GRID_W
    return pl.pallas_call(
        functools.partial(_na_lat_kernel, nblk=nblk),
        grid=(w // LANES, b),
        in_specs=[pl.BlockSpec((1, t, LANES), lambda h, i: (i, 0, h)),
                  pl.BlockSpec((1, t, LANES), lambda h, i: (i, 0, h)),
                  pl.BlockSpec((1, t // MXU_DIM, LANES, MXU_DIM), lambda h, i: (i, 0, h, 0)),
                  pl.BlockSpec((1, nctx, LANES), lambda h, i: (i, 0, h)),
                  pl.BlockSpec((1, LANES, nctx), lambda h, i: (i, h, 0)),
                  pl.BlockSpec((2, 3, nwin, tq), lambda h, i: (h, 0, 0, 0)),
                  pl.BlockSpec((1, t, LANES), lambda h, i: (i, 0, h))],
        out_specs=pl.BlockSpec((1, t, LANES), lambda h, i: (i, 0, h)),
        out_shape=jax.ShapeDtypeStruct((b, t, w), BF16),
        scratch_shapes=[pltpu.VMEM((4, nwin + nctx, tq), F32), pltpu.VMEM((2, LANES, tq), F32),
                        pltpu.VMEM((2, SUBLANES, tq), F32)],
        compiler_params=_params(("parallel", "parallel")),
        name="na_lat_attention",
    )(q, k, vt, kc, vct, bias, sg)


def _na_bias_tables(rel_bias, rows):
    nblk = rows // NA_QROWS
    cols = np.arange(GRID_W)
    col_start = np.clip(cols - NA_COLS // 2, 0, GRID_W - NA_COLS)
    col_ok = (cols[:, None] >= col_start[None, :]) & (cols[:, None] < col_start[None, :] + NA_COLS)
    dc = np.clip(cols[:, None] - cols[None, :] + NA_COLS - 1, 0, 2 * NA_COLS - 2)
    dr = np.zeros((3, NA_WIN_ROWS, NA_QROWS), np.int32)
    ok = np.zeros((3, NA_WIN_ROWS, NA_QROWS), bool)
    for kind, blk in enumerate((0, 1, nblk - 1)):
        r0 = blk * NA_QROWS
        ws = int(np.clip(r0 - NA_MAX_ROWS // 2, 0, rows - NA_WIN_ROWS))
        for j in range(NA_WIN_ROWS):
            for i in range(NA_QROWS):
                rs = int(np.clip(r0 + i - NA_MAX_ROWS // 2, 0, rows - NA_MAX_ROWS))
                ok[kind, j, i] = rs <= ws + j < rs + NA_MAX_ROWS
                dr[kind, j, i] = np.clip(ws + j - (r0 + i) + NA_MAX_ROWS - 1, 0, 2 * NA_MAX_ROWS - 2)
    nh, n_dr, n_dc = rel_bias.shape
    half = LANES // GRID_W
    return pl.pallas_call(
        functools.partial(_na_bias_kernel, n_dr=n_dr, n_dc=n_dc, dr_idx=dr.tolist(), ok=ok.tolist()),
        grid=(nh,),
        in_specs=[pl.BlockSpec(memory_space=pltpu.SMEM),
                  pl.BlockSpec((GRID_W, LANES), lambda h: (0, 0)),
                  pl.BlockSpec((GRID_W, LANES), lambda h: (0, 0))],
        out_specs=pl.BlockSpec((1, 3, NA_WIN_ROWS * GRID_W, NA_QROWS * GRID_W), lambda h: (h, 0, 0, 0)),
        out_shape=jax.ShapeDtypeStruct((nh, 3, NA_WIN_ROWS * GRID_W, NA_QROWS * GRID_W), F32),
        scratch_shapes=[pltpu.VMEM((n_dr, GRID_W, NA_QROWS * GRID_W), F32)],
        compiler_params=_params(("parallel",)),
        name="na_bias_tables",
    )(rel_bias.reshape(-1).astype(F32),
      jnp.asarray(np.tile(dc, (1, half)), jnp.int32),
      jnp.asarray(np.tile(col_ok, (1, half)), jnp.int32))


def _na_bias_kernel(b_ref, dc_ref, colok_ref, o_ref, tt_scr, *, n_dr, n_dc, dr_idx, ok):
    base = pl.program_id(0) * (n_dr * n_dc)
    dcb = dc_ref[...]
    colok = colok_ref[...] > 0
    for d in range(n_dr):
        acc = jnp.zeros(dcb.shape, F32)
        for e in range(n_dc):
            acc = jnp.where(dcb == e, b_ref[base + d * n_dc + e], acc)
        tile = jnp.where(colok, acc * LOG2E, -jnp.inf)
        tt_scr[d] = jnp.concatenate([tile] * (tt_scr.shape[2] // LANES), axis=1)
    shape = tt_scr.shape[1:]
    qrow = lax.broadcasted_iota(jnp.int32, shape, 1) // GRID_W
    neg = jnp.full(shape, -jnp.inf, F32)
    for kind in range(3):
        for j in range(NA_WIN_ROWS):
            band = neg
            for i in range(NA_QROWS):
                if ok[kind][j][i]:
                    band = jnp.where(qrow == i, tt_scr[dr_idx[kind][j][i]], band)
            o_ref[0, kind, j * GRID_W:(j + 1) * GRID_W, :] = band


def _rotate_half_cols(w):
    w4 = w.reshape(w.shape[:-1] + (2, 2, ROPE_AXIS_FREQS))
    return jnp.stack([-w4[..., 1, :], w4[..., 0, :]], axis=-2).reshape(w.shape)


def _rope_tables(n):
    t = np.arange(n)
    pos = np.stack([t // GRID_W, t % GRID_W], axis=-1).astype(np.float64)
    inv = ROPE_THETA ** (-np.arange(ROPE_AXIS_FREQS, dtype=np.float64) / ROPE_AXIS_FREQS)
    ang = pos[:, :, None] * inv
    cos, sin = np.cos(ang), np.sin(ang)
    zeros = np.zeros((n, LANES - QK_ROPE_DIM))
    cos2 = np.concatenate([cos[:, 0], cos[:, 0], cos[:, 1], cos[:, 1], zeros], axis=-1)
    sin2 = np.concatenate([sin[:, 0], sin[:, 0], sin[:, 1], sin[:, 1], zeros], axis=-1)
    return jnp.asarray(cos2, F32), jnp.asarray(sin2, F32)


def kernel(x_prompt, x_sample, cache_mla_ckv, cache_mla_krope, cache_na_k, cache_na_v, c, c_ctx,
           w_ada, b_ada, pre_norm_g, post_norm_g, mla_w_in, mla_q_norm_g, mla_w_qb, mla_kv_norm_g,
           mla_w_kvb, mla_w_out, na_w_in, na_rel_bias, na_w_out):
    bp, tp, d = x_prompt.shape
    bs, ts, _ = x_sample.shape
    tm = 512
    tmp = min(tm, tp)

    cond = jnp.zeros((MOD_ROWS, d), F32).at[:bs].set(c).at[CTX_ROW].set(c_ctx)
    mod = _modulation(cond, w_ada, b_ada)
    mod = mod.reshape(mod.shape[0], MOD_ROWS, 1, 3 * d)

    w_in = mla_w_in[0]
    n_small = Q_LORA_RANK + KV_LORA_RANK + QK_ROPE_DIM
    w_in_ext = jnp.concatenate(
        [w_in[:, :n_small], _rotate_half_cols(w_in[:, n_small - QK_ROPE_DIM:n_small]), w_in[:, n_small:]],
        axis=1).astype(BF16)
    wqb3 = mla_w_qb[0].reshape(Q_LORA_RANK, MLA_HEADS, QK_NOPE_DIM + QK_ROPE_DIM)
    wqb_ext = jnp.concatenate([wqb3, _rotate_half_cols(wqb3[..., QK_NOPE_DIM:])], axis=-1)
    wqb_ext = wqb_ext.reshape(Q_LORA_RANK, MLA_HEADS * 256).astype(BF16)
    wkvb3 = mla_w_kvb[0].reshape(KV_LORA_RANK, MLA_HEADS, QK_NOPE_DIM + V_HEAD_DIM)
    wkb = wkvb3[..., :QK_NOPE_DIM].transpose(1, 0, 2).astype(BF16)
    wvbt = wkvb3[..., QK_NOPE_DIM:].transpose(1, 2, 0).astype(BF16)
    preg0 = pre_norm_g[0].reshape(1, d)
    postg0 = post_norm_g[0].reshape(1, d)
    qg = mla_q_norm_g[0].reshape(1, Q_LORA_RANK)
    kvg = mla_kv_norm_g[0].reshape(1, KV_LORA_RANK)
    cos_s, sin_s = _rope_tables(ts)
    pad = jnp.zeros((tmp, LANES - QK_ROPE_DIM), F32)
    cos_p = jnp.concatenate([jnp.ones((tmp, QK_ROPE_DIM), F32), pad], axis=-1)
    sin_p = jnp.zeros((tmp, LANES), F32)
    w_out0 = mla_w_out[0].astype(BF16)

    qp, ckvb_p, krb_p, sg_p, ckv_p, kr_p = _mla_in(
        x_prompt, mod[0], CTX_ROW, preg0, w_in_ext, qg, wqb_ext, kvg, cos_p, sin_p, True, tmp)
    qs, ckvb_s, krb_s, sg_s = _mla_in(
        x_sample, mod[0], None, preg0, w_in_ext, qg, wqb_ext, kvg, cos_s, sin_s, False, tm)

    op = _mla_ctx_attn(qp, ckvb_p, krb_p, wkb, wvbt, sg_p)
    osm = _mla_attn(qs, cache_mla_ckv, cache_mla_krope, 0, ckvb_s, krb_s, wkb, wvbt, sg_s)

    xp = _out_proj(op, x_prompt, mod[0], CTX_ROW, w_out0, postg0, tmp)
    xs = _out_proj(osm, x_sample, mod[0], None, w_out0, postg0, tm)

    preg1 = pre_norm_g[1].reshape(1, d)
    postg1 = post_norm_g[1].reshape(1, d)
    na_win = na_w_in[0].astype(BF16)
    w_out1 = na_w_out[0].astype(BF16)
    q1p, k1p, vt1p, sg1p, k_state, v_state = _na_in(xp, mod[1], CTX_ROW, preg1, na_win, True, tmp)
    q1s, k1s, vt1s, sg1s = _na_in(xs, mod[1], None, preg1, na_win, False, tm)
    o1p = _na_ctx(q1p, k1p, vt1p, sg1p)
    npast = cache_na_k.shape[2]
    kc = cache_na_k[:, 0].reshape(bs, npast, NA_WIDTH).astype(BF16)
    vct = cache_na_v[:, 0].reshape(bs, npast, NA_WIDTH).transpose(0, 2, 1).astype(BF16)
    bias = _na_bias_tables(na_rel_bias[0], ts // GRID_W)
    o1s = _na_lat(q1s, k1s, vt1s, kc, vct, bias, sg1s)
    yp = _out_proj(o1p, xp, mod[1], CTX_ROW, w_out1, postg1, tmp)
    ys = _out_proj(o1s, xs, mod[1], None, w_out1, postg1, tm)

    return (yp, ys,
            ckv_p.reshape(bp, 1, tp, KV_LORA_RANK),
            kr_p.reshape(bp, 1, tp, QK_ROPE_DIM),
            k_state.reshape(bp, 1, tp, NA_HEADS, NA_HEAD_DIM),
            v_state.reshape(bp, 1, tp, NA_HEADS, NA_HEAD_DIM))
```

```python
import functools
import math

import numpy as np
import jax
import jax.numpy as jnp
from jax import lax
from jax.experimental import pallas as pl
from jax.experimental.pallas import tpu as pltpu

F32 = jnp.float32
BF16 = jnp.bfloat16

D_MODEL = 1024
GRID_W = 64
MLA_HEADS = 16
Q_LORA_RANK = 256
KV_LORA_RANK = 128
QK_NOPE_DIM = 128
QK_ROPE_DIM = 64
V_HEAD_DIM = 128
MLA_WIDTH = MLA_HEADS * V_HEAD_DIM
ROPE_AXIS_FREQS = QK_ROPE_DIM // 4
ROPE_THETA = 10000.0
NA_HEADS = 16
NA_HEAD_DIM = 64
NA_WIDTH = NA_HEADS * NA_HEAD_DIM
NA_MAX_ROWS = 8
NA_COLS = 16
EPS = 1e-6
LOG2E = math.log2(math.e)
MLA_QSCALE = (QK_NOPE_DIM + QK_ROPE_DIM) ** -0.5 * LOG2E
NA_QSCALE = NA_HEAD_DIM ** -0.5 * LOG2E

LANES = 128
SUBLANES = 8
BF16_ROWS = 16
NA_UNROLL = 6
CTX_GROUP = 4
MXU_DIM = 256
MOD_ROWS = 16
CTX_ROW = 8
VMEM_LIMIT = 48 * 1024 * 1024

NA_QROWS = 4
NA_WIN_ROWS = NA_QROWS + NA_MAX_ROWS

_NT = (((1,), (1,)), ((), ()))


def _params(sem, vmem=VMEM_LIMIT):
    return pltpu.CompilerParams(dimension_semantics=sem, vmem_limit_bytes=vmem)


def _rms(x, g):
    return x * lax.rsqrt(jnp.mean(x * x, axis=-1, keepdims=True) + EPS) * g


def _silu(x):
    return x * jax.nn.sigmoid(x)


def _mod_kernel(cond_ref, w_ref, b_ref, o_ref):
    o_ref[0] = jnp.dot(_silu(cond_ref[...]), w_ref[0], preferred_element_type=F32) + b_ref[0]


def _modulation(cond, w_ada, b_ada):
    depth, d, n = w_ada.shape
    bn = 768
    return pl.pallas_call(
        _mod_kernel,
        grid=(depth, n // bn),
        in_specs=[pl.BlockSpec((MOD_ROWS, d), lambda i, j: (0, 0)),
                  pl.BlockSpec((1, d, bn), lambda i, j: (i, 0, j)),
                  pl.BlockSpec((1, 1, bn), lambda i, j: (i, 0, j))],
        out_specs=pl.BlockSpec((1, MOD_ROWS, bn), lambda i, j: (i, 0, j)),
        out_shape=jax.ShapeDtypeStruct((depth, MOD_ROWS, n), F32),
        compiler_params=_params(("parallel", "parallel")),
        name="modulation",
    )(cond, w_ada, b_ada.reshape(depth, 1, n))


def _modulated(x, mod, g):
    return _rms(x, g) * (1.0 + mod[:, D_MODEL:2 * D_MODEL]) + mod[:, :D_MODEL]


def _mla_in_kernel(x_ref, mod_ref, preg_ref, win_ref, qg_ref, wqb_ref, kvg_ref, cos_ref, sin_ref,
                   q_ref, ckvb_ref, krb_ref, sg_ref, *state_refs):
    h = _modulated(x_ref[0], mod_ref[0], preg_ref[...]).astype(BF16)
    ya = jnp.dot(h, win_ref[:, :512], preferred_element_type=F32)
    ckv = _rms(ya[:, 256:384], kvg_ref[...])
    kr2 = ya[:, 384:512]
    cos2 = cos_ref[...]
    sin2 = sin_ref[...]
    ckvb_ref[0] = ckv.astype(BF16)
    krb_ref[0] = (kr2 * cos2 + pltpu.roll(kr2, 64, 1) * sin2).astype(BF16)
    if state_refs:
        state_refs[0][0] = ckv
        state_refs[1][0] = kr2[:, :QK_ROPE_DIM]
    qn = _rms(ya[:, :256], qg_ref[...]).astype(BF16)
    for hh in range(MLA_HEADS):
        yq = jnp.dot(qn, wqb_ref[:, hh * 256:(hh + 1) * 256], preferred_element_type=F32)
        sec = yq[:, 128:]
        q_ref[0, hh, :, 0:128] = (yq[:, :128] * MLA_QSCALE).astype(BF16)
        q_ref[0, hh, :, 128:256] = ((sec * cos2 + pltpu.roll(sec, 64, 1) * sin2) * MLA_QSCALE).astype(BF16)
    for c in range(MLA_WIDTH // 512):
        g = jnp.dot(h, win_ref[:, 512 + c * 512:1024 + c * 512], preferred_element_type=F32)
        sg_ref[0, :, c * 512:(c + 1) * 512] = _silu(g).astype(BF16)


def _mla_in(x, mod, mod_row, preg, win, qg, wqb, kvg, cos2, sin2, with_state, tm=512):
    b, t, d = x.shape
    nt = t // tm
    rope_blocks = cos2.shape[0] // tm
    tbl_idx = (lambda i, j: (j, 0)) if rope_blocks > 1 else (lambda i, j: (0, 0))
    mod_idx = (lambda i, j: (i, 0, 0)) if mod_row is None else (lambda i, j: (mod_row, 0, 0))
    const = lambda i, j: (0, 0)
    out_shape = [jax.ShapeDtypeStruct((b, MLA_HEADS, t, 256), BF16),
                 jax.ShapeDtypeStruct((b, t, 128), BF16),
                 jax.ShapeDtypeStruct((b, t, 128), BF16),
                 jax.ShapeDtypeStruct((b, t, MLA_WIDTH), BF16)]
    out_specs = [pl.BlockSpec((1, MLA_HEADS, tm, 256), lambda i, j: (i, 0, j, 0)),
                 pl.BlockSpec((1, tm, 128), lambda i, j: (i, j, 0)),
                 pl.BlockSpec((1, tm, 128), lambda i, j: (i, j, 0)),
                 pl.BlockSpec((1, tm, MLA_WIDTH), lambda i, j: (i, j, 0))]
    if with_state:
        out_shape += [jax.ShapeDtypeStruct((b, t, KV_LORA_RANK), F32),
                      jax.ShapeDtypeStruct((b, t, QK_ROPE_DIM), F32)]
        out_specs += [pl.BlockSpec((1, tm, KV_LORA_RANK), lambda i, j: (i, j, 0)),
                      pl.BlockSpec((1, tm, QK_ROPE_DIM), lambda i, j: (i, j, 0))]
    return pl.pallas_call(
        _mla_in_kernel,
        grid=(b, nt),
        in_specs=[pl.BlockSpec((1, tm, d), lambda i, j: (i, j, 0)),
                  pl.BlockSpec((1, 1, 3 * d), mod_idx),
                  pl.BlockSpec((1, d), const),
                  pl.BlockSpec(win.shape, const, pipeline_mode=pl.Buffered(1)),
                  pl.BlockSpec((1, Q_LORA_RANK), const),
                  pl.BlockSpec(wqb.shape, const, pipeline_mode=pl.Buffered(1)),
                  pl.BlockSpec((1, KV_LORA_RANK), const),
                  pl.BlockSpec((tm, 128), tbl_idx),
                  pl.BlockSpec((tm, 128), tbl_idx)],
        out_specs=out_specs,
        out_shape=out_shape,
        compiler_params=_params(("parallel", "parallel")),
        name="mla_in_proj",
    )(x, mod, preg, win, qg, wqb, kvg, cos2, sin2)


def _rows8(x, op):
    return op(x.reshape(x.shape[0] // SUBLANES, SUBLANES, x.shape[1]), axis=0)


def _pv_with_colsum(vts, ps):
    acc = None
    for vt, p in zip(vts, ps):
        lhs = jnp.concatenate([vt, jnp.ones((BF16_ROWS, vt.shape[1]), BF16)], axis=0)
        part = jnp.dot(lhs, p, preferred_element_type=F32)
        acc = part if acc is None else acc + part
    nd = vts[0].shape[0]
    return acc[:nd], acc[nd:nd + 1]


def _mla_attn_kernel(q_ref, cckv_ref, ckr_ref, ckv_ref, kr_ref, wkb_ref, wvbt_ref, sg_ref, o_ref,
                     k_scr, vt_scr, s_scr, acc_scr, l_scr, *, tq, unroll):
    nctx = cckv_ref.shape[2]
    nk = nctx + ckv_ref.shape[1]
    chunks = [slice(lo, lo + MXU_DIM) for lo in range(0, nk, MXU_DIM)]
    for rows in chunks:
        if rows.start < nctx:
            c = cckv_ref[0, 0, rows, :].astype(BF16)
            k_scr[rows, 128:128 + QK_ROPE_DIM] = ckr_ref[0, 0, rows, :].astype(BF16)
            k_scr[rows, 128 + QK_ROPE_DIM:256] = jnp.zeros((MXU_DIM, 128 - QK_ROPE_DIM), BF16)
        else:
            lat = slice(rows.start - nctx, rows.stop - nctx)
            c = ckv_ref[0, lat, :]
            k_scr[rows, 128:256] = kr_ref[0, lat, :]
        k_scr[rows, 0:128] = jnp.dot(c, wkb_ref[0], preferred_element_type=F32).astype(BF16)
        vt_scr[:, rows] = lax.dot_general(wvbt_ref[0], c, _NT, preferred_element_type=F32).astype(BF16)

    def scores(qi, slot):
        q = q_ref[0, 0, pl.ds(pl.multiple_of(qi * tq, tq), tq), :]
        m8 = None
        for rows in chunks:
            st = lax.dot_general(k_scr[rows, :], q, _NT, preferred_element_type=F32)
            s_scr[slot, rows, :] = st
            cm = _rows8(st, jnp.max)
            m8 = cm if m8 is None else jnp.maximum(m8, cm)
        return jnp.max(m8, axis=0, keepdims=True)

    def finish(qi, slot, m):
        l8 = jnp.zeros((SUBLANES, tq), F32)
        acc = jnp.zeros((V_HEAD_DIM, tq), F32)
        for rows in chunks:
            p = jnp.exp2(s_scr[slot, rows, :] - m)
            l8 = l8 + _rows8(p, jnp.sum)
            acc = acc + jnp.dot(vt_scr[:, rows], p.astype(BF16), preferred_element_type=F32)
        acc_scr[slot] = acc
        l_scr[slot] = l8

    def writeout(qi, slot):
        l = jnp.sum(l_scr[slot], axis=0, keepdims=True)
        rows = pl.ds(pl.multiple_of(qi * tq, tq), tq)
        o_ref[0, rows, :] = ((acc_scr[slot] / l).T * sg_ref[0, rows, :].astype(F32)).astype(BF16)

    def stages(first, n, parity, m):
        for t in range(n):
            par = (parity + t) % 2
            m_next = scores(first + t + 1, 1 - par)
            finish(first + t, par, m)
            writeout(first + t - 1, 1 - par)
            m = m_next
        return m

    nq = q_ref.shape[2] // tq
    m0 = scores(0, 0)
    m = scores(1, 1)
    finish(0, 0, m0)
    trips, rest = divmod(nq - 2, unroll)
    m = lax.fori_loop(0, trips, lambda j, m: stages(unroll * j + 1, unroll, 1, m), m)
    m = stages(trips * unroll + 1, rest, 1, m)
    finish(nq - 1, (nq - 1) % 2, m)
    writeout(nq - 2, nq % 2)
    writeout(nq - 1, (nq - 1) % 2)


def _mla_attn(q, cache_ckv, cache_kr, layer, ckv, kr, wkb, wvbt, sg, tq=256, unroll=6):
    b, nh, t, _ = q.shape
    nctx = cache_ckv.shape[2]
    nk = nctx + t
    assert nctx % MXU_DIM == 0 and t % (2 * tq) == 0 and unroll % 2 == 0
    return pl.pallas_call(
        functools.partial(_mla_attn_kernel, tq=tq, unroll=unroll),
        grid=(b, nh),
        in_specs=[pl.BlockSpec((1, 1, t, 256), lambda i, h: (i, h, 0, 0)),
                  pl.BlockSpec((1, 1, nctx, KV_LORA_RANK), lambda i, h: (i, layer, 0, 0)),
                  pl.BlockSpec((1, 1, nctx, QK_ROPE_DIM), lambda i, h: (i, layer, 0, 0)),
                  pl.BlockSpec((1, t, 128), lambda i, h: (i, 0, 0)),
                  pl.BlockSpec((1, t, 128), lambda i, h: (i, 0, 0)),
                  pl.BlockSpec((1, 128, 128), lambda i, h: (h, 0, 0)),
                  pl.BlockSpec((1, 128, 128), lambda i, h: (h, 0, 0)),
                  pl.BlockSpec((1, t, V_HEAD_DIM), lambda i, h: (i, 0, h))],
        out_specs=pl.BlockSpec((1, t, V_HEAD_DIM), lambda i, h: (i, 0, h)),
        out_shape=jax.ShapeDtypeStruct((b, t, MLA_WIDTH), BF16),
        scratch_shapes=[pltpu.VMEM((nk, 256), BF16), pltpu.VMEM((V_HEAD_DIM, nk), BF16),
                        pltpu.VMEM((2, nk, tq), F32), pltpu.VMEM((2, V_HEAD_DIM, tq), F32),
                        pltpu.VMEM((2, SUBLANES, tq), F32)],
        compiler_params=_params(("parallel", "parallel")),
        name="mla_attention",
    )(q, cache_ckv, cache_kr, ckv, kr, wkb, wvbt, sg)


def _mla_ctx_kernel(q_ref, ckv_ref, kr_ref, wkb_ref, wvbt_ref, sg_ref, o_ref):
    c = ckv_ref[0]
    kr = kr_ref[0]
    nh = q_ref.shape[1]

    def expand(h):
        k_nope = jnp.dot(c, wkb_ref[h], preferred_element_type=F32).astype(BF16)
        vt = lax.dot_general(wvbt_ref[h], c, _NT, preferred_element_type=F32).astype(BF16)
        return jnp.concatenate([k_nope, kr], axis=1), vt

    def scores(h, k):
        st = lax.dot_general(k, q_ref[0, h], _NT, preferred_element_type=F32)
        return st, jnp.max(_rows8(st, jnp.max), axis=0, keepdims=True)

    def finish(h, st, m, vt):
        p = jnp.exp2(st - m)
        l = jnp.sum(_rows8(p, jnp.sum), axis=0, keepdims=True)
        ot = jnp.dot(vt, p.astype(BF16), preferred_element_type=F32) / l
        cols = slice(h * V_HEAD_DIM, (h + 1) * V_HEAD_DIM)
        o_ref[0, :, cols] = (ot.T * sg_ref[0, :, cols].astype(F32)).astype(BF16)

    groups = [range(g, min(g + CTX_GROUP, nh)) for g in range(0, nh, CTX_GROUP)]
    ng = len(groups)
    kv = {g: [expand(h) for h in groups[g]] for g in range(min(2, ng))}
    sm = {0: [scores(h, k) for h, (k, _) in zip(groups[0], kv[0])]}
    for g in range(ng):
        if g + 2 < ng:
            kv[g + 2] = [expand(h) for h in groups[g + 2]]
        if g + 1 < ng:
            sm[g + 1] = [scores(h, k) for h, (k, _) in zip(groups[g + 1], kv[g + 1])]
        for h, (st, m), (_, vt) in zip(groups[g], sm.pop(g), kv.pop(g)):
            finish(h, st, m, vt)


def _mla_ctx_attn(q, ckv, kr, wkb, wvbt, sg):
    b, nh, t, _ = q.shape
    const3 = lambda i: (0, 0, 0)
    return pl.pallas_call(
        _mla_ctx_kernel,
        grid=(b,),
        in_specs=[pl.BlockSpec((1, nh, t, 256), lambda i: (i, 0, 0, 0)),
                  pl.BlockSpec((1, t, 128), lambda i: (i, 0, 0)),
                  pl.BlockSpec((1, t, 128), lambda i: (i, 0, 0)),
                  pl.BlockSpec(wkb.shape, const3),
                  pl.BlockSpec(wvbt.shape, const3),
                  pl.BlockSpec((1, t, MLA_WIDTH), lambda i: (i, 0, 0))],
        out_specs=pl.BlockSpec((1, t, MLA_WIDTH), lambda i: (i, 0, 0)),
        out_shape=jax.ShapeDtypeStruct((b, t, MLA_WIDTH), BF16),
        compiler_params=_params(("parallel",)),
        name="mla_ctx_attention",
    )(q, ckv, kr, wkb, wvbt, sg)


def _out_kernel(og_ref, x_ref, mod_ref, wout_ref, postg_ref, y_ref):
    out = jnp.dot(og_ref[0], wout_ref[...], preferred_element_type=F32)
    gate = mod_ref[0][:, 2 * D_MODEL:]
    y_ref[0] = x_ref[0] + gate * _rms(out, postg_ref[...])


def _out_proj(og, x, mod, mod_row, wout, postg, tm=512):
    b, t, d = x.shape
    w = og.shape[-1]
    mod_idx = (lambda i, j: (i, 0, 0)) if mod_row is None else (lambda i, j: (mod_row, 0, 0))
    const = lambda i, j: (0, 0)
    return pl.pallas_call(
        _out_kernel,
        grid=(b, t // tm),
        in_specs=[pl.BlockSpec((1, tm, w), lambda i, j: (i, j, 0)),
                  pl.BlockSpec((1, tm, d), lambda i, j: (i, j, 0)),
                  pl.BlockSpec((1, 1, 3 * d), mod_idx),
                  pl.BlockSpec(wout.shape, const),
                  pl.BlockSpec((1, d), const)],
        out_specs=pl.BlockSpec((1, tm, d), lambda i, j: (i, j, 0)),
        out_shape=jax.ShapeDtypeStruct((b, t, d), F32),
        compiler_params=_params(("parallel", "parallel")),
        name="out_proj",
    )(og, x, mod, wout, postg)


def _na_in_kernel(x_ref, mod_ref, preg_ref, win_ref, q_ref, k_ref, vt_ref, sg_ref, *state_refs):
    h = _modulated(x_ref[0], mod_ref[0], preg_ref[...]).astype(BF16)
    w = NA_WIDTH
    q = jnp.dot(h, win_ref[:, 0:w], preferred_element_type=F32)
    q_ref[0] = (q * NA_QSCALE).astype(BF16)
    k = jnp.dot(h, win_ref[:, w:2 * w], preferred_element_type=F32)
    k_ref[0] = k.astype(BF16)
    v = jnp.dot(h, win_ref[:, 2 * w:3 * w], preferred_element_type=F32)
    vt = v.T.astype(BF16)
    for ci in range(vt_ref.shape[1]):
        vt_ref[0, ci] = vt[:, ci * MXU_DIM:(ci + 1) * MXU_DIM]
    if state_refs:
        state_refs[0][0] = k
        state_refs[1][0] = v
    g = jnp.dot(h, win_ref[:, 3 * w:4 * w], preferred_element_type=F32)
    sg_ref[0] = _silu(g).astype(BF16)


def _na_in(x, mod, mod_row, preg, win, with_state, tm):
    b, t, d = x.shape
    w = NA_WIDTH
    mod_idx = (lambda i, j: (i, 0, 0)) if mod_row is None else (lambda i, j: (mod_row, 0, 0))
    const = lambda i, j: (0, 0)
    row_blk = pl.BlockSpec((1, tm, w), lambda i, j: (i, j, 0))
    out_shape = [jax.ShapeDtypeStruct((b, t, w), BF16), jax.ShapeDtypeStruct((b, t, w), BF16),
                 jax.ShapeDtypeStruct((b, t // MXU_DIM, w, MXU_DIM), BF16),
                 jax.ShapeDtypeStruct((b, t, w), BF16)]
    out_specs = [row_blk, row_blk,
                 pl.BlockSpec((1, tm // MXU_DIM, w, MXU_DIM), lambda i, j: (i, j, 0, 0)), row_blk]
    if with_state:
        out_shape += [jax.ShapeDtypeStruct((b, t, w), F32), jax.ShapeDtypeStruct((b, t, w), F32)]
        out_specs += [row_blk, row_blk]
    return pl.pallas_call(
        _na_in_kernel,
        grid=(b, t // tm),
        in_specs=[pl.BlockSpec((1, tm, d), lambda i, j: (i, j, 0)),
                  pl.BlockSpec((1, 1, 3 * d), mod_idx),
                  pl.BlockSpec((1, d), const),
                  pl.BlockSpec(win.shape, const)],
        out_specs=out_specs,
        out_shape=out_shape,
        compiler_params=_params(("parallel", "parallel")),
        name="na_in_proj",
    )(x, mod, preg, win)


def _head_masks(shape):
    lane = lax.broadcasted_iota(jnp.int32, shape, 1)
    return (lane < NA_HEAD_DIM, lane >= NA_HEAD_DIM)


def _na_ctx_kernel(q_ref, k_ref, vt_ref, sg_ref, o_ref):
    masks = _head_masks((q_ref.shape[1], LANES))
    nhead = q_ref.shape[2] // NA_HEAD_DIM

    def scores(h):
        cols = slice(h // 2 * LANES, (h // 2 + 1) * LANES)
        q2 = q_ref[0, :, cols]
        qm = jnp.where(masks[h % 2], q2, jnp.zeros_like(q2))
        st = lax.dot_general(k_ref[0, :, cols], qm, _NT, preferred_element_type=F32)
        return st, jnp.max(_rows8(st, jnp.max), axis=0, keepdims=True)

    def finish(h, st, m):
        p = jnp.exp2(st - m).astype(BF16)
        ot, l = _pv_with_colsum([vt_ref[0, 0, h * NA_HEAD_DIM:(h + 1) * NA_HEAD_DIM, :]], [p])
        return ot / l

    def store(hp, outs):
        cols = slice(hp * LANES, (hp + 1) * LANES)
        o2 = jnp.concatenate(outs, axis=0).T * sg_ref[0, :, cols].astype(F32)
        o_ref[0, :, cols] = o2.astype(BF16)

    groups = [range(g, g + CTX_GROUP) for g in range(0, nhead, CTX_GROUP)]
    nxt = [scores(h) for h in groups[0]]
    for gi, group in enumerate(groups):
        cur, nxt = nxt, ([scores(h) for h in groups[gi + 1]] if gi + 1 < len(groups) else None)
        outs = [finish(h, *sm) for h, sm in zip(group, cur)]
        for hp in range(group[0] // 2, group[-1] // 2 + 1):
            store(hp, outs[2 * hp - group[0]:2 * hp - group[0] + 2])


def _na_ctx(q, k, vt, sg):
    b, t, w = q.shape
    return pl.pallas_call(
        _na_ctx_kernel,
        grid=(b,),
        in_specs=[pl.BlockSpec((1, t, w), lambda i: (i, 0, 0)),
                  pl.BlockSpec((1, t, w), lambda i: (i, 0, 0)),
                  pl.BlockSpec((1, t // MXU_DIM, w, MXU_DIM), lambda i: (i, 0, 0, 0)),
                  pl.BlockSpec((1, t, w), lambda i: (i, 0, 0))],
        out_specs=pl.BlockSpec((1, t, w), lambda i: (i, 0, 0)),
        out_shape=jax.ShapeDtypeStruct((b, t, w), BF16),
        compiler_params=_params(("parallel",)),
        name="na_ctx_attention",
    )(q, k, vt, sg)


def _na_lat_kernel(q_ref, k_ref, vt_ref, kc_ref, vct_ref, bias_ref, sg_ref, o_ref,
                   s_scr, ot_scr, l_scr, *, nblk):
    tq = NA_QROWS * GRID_W
    nwin = NA_WIN_ROWS * GRID_W
    nctx = kc_ref.shape[1]
    masks = _head_masks((tq, LANES))

    def window(blk):
        ws = jnp.clip(blk * NA_QROWS - NA_MAX_ROWS // 2, 0, nblk * NA_QROWS - NA_WIN_ROWS)
        return pl.multiple_of(ws * GRID_W, MXU_DIM)

    def scores(blk, par, heads=(0, 1)):
        kind = jnp.where(blk == 0, 0, jnp.where(blk == nblk - 1, 2, 1))
        q2 = q_ref[0, pl.ds(pl.multiple_of(blk * tq, tq), tq), :]
        kwin = k_ref[0, pl.ds(window(blk), nwin), :]
        ms = []
        for hl in heads:
            qm = jnp.where(masks[hl], q2, jnp.zeros_like(q2))
            s_loc = lax.dot_general(kwin, qm, _NT, preferred_element_type=F32) + bias_ref[hl, kind]
            s_ctx = lax.dot_general(kc_ref[0], qm, _NT, preferred_element_type=F32)
            s_scr[2 * par + hl, 0:nwin, :] = s_loc
            s_scr[2 * par + hl, nwin:nwin + nctx, :] = s_ctx
            m8 = jnp.maximum(_rows8(s_loc, jnp.max), _rows8(s_ctx, jnp.max))
            ms.append(jnp.max(m8, axis=0, keepdims=True))
        return tuple(ms)

    def finish(blk, par, ms, heads=(0, 1)):
        c0 = window(blk) // MXU_DIM
        for hl in heads:
            rows = slice(hl * NA_HEAD_DIM, (hl + 1) * NA_HEAD_DIM)
            vts = [vt_ref[0, c0 + ci, rows, :] for ci in range(nwin // MXU_DIM)] + [vct_ref[0, rows, :]]
            edges = list(range(0, nwin + 1, MXU_DIM)) + [nwin + nctx]
            ps = [jnp.exp2(s_scr[2 * par + hl, lo:hi, :] - ms[hl]).astype(BF16)
                  for lo, hi in zip(edges[:-1], edges[1:])]
            ot, l = _pv_with_colsum(vts, ps)
            ot_scr[par, rows, :] = ot
            l_scr[par, hl:hl + 1, :] = l

    def writeout(blk, par):
        lrows = jnp.concatenate([jnp.broadcast_to(l_scr[par, hl:hl + 1, :], (NA_HEAD_DIM, tq))
                                 for hl in range(2)], axis=0)
        rows = pl.ds(pl.multiple_of(blk * tq, tq), tq)
        o_ref[0, rows, :] = ((ot_scr[par] / lrows).T * sg_ref[0, rows, :].astype(F32)).astype(BF16)

    def stage(i, par, ms):
        ms_next = ()
        for hl in range(2):
            ms_next += scores(i + 1, 1 - par, (hl,))
            finish(i, par, ms, (hl,))
            if hl == 0:
                writeout(i - 1, 1 - par)
        return ms_next

    def stages(first, n, ms):
        for t in range(n):
            ms = stage(first + t, (1 + t) % 2, ms)
        return ms

    ms0 = scores(0, 0)
    ms1 = scores(1, 1)
    finish(0, 0, ms0)
    trips, rest = divmod(nblk - 2, NA_UNROLL)
    ms = lax.fori_loop(0, trips, lambda j, ms: stages(NA_UNROLL * j + 1, NA_UNROLL, ms), ms1)
    ms_last = stages(trips * NA_UNROLL + 1, rest, ms)
    finish(nblk - 1, 1, ms_last)
    writeout(nblk - 2, 0)
    writeout(nblk - 1, 1)


def _na_lat(q, k, vt, kc, vct, bias, sg):
    b, t, w = q.shape
    tq = NA_QROWS * GRID_W
    nblk = t // tq
    nctx = kc.shape[1]
    nwin = NA_WIN_ROWS * GRID_W
    return pl.pallas_call(
        functools.partial(_na_lat_kernel, nblk=nblk),
        grid=(w // LANES, b),
        in_specs=[pl.BlockSpec((1, t, LANES), lambda h, i: (i, 0, h)),
                  pl.BlockSpec((1, t, LANES), lambda h, i: (i, 0, h)),
                  pl.BlockSpec((1, t // MXU_DIM, LANES, MXU_DIM), lambda h, i: (i, 0, h, 0)),
                  pl.BlockSpec((1, nctx, LANES), lambda h, i: (i, 0, h)),
                  pl.BlockSpec((1, LANES, nctx), lambda h, i: (i, h, 0)),
                  pl.BlockSpec((2, 3, nwin, tq), lambda h, i: (h, 0, 0, 0)),
                  pl.BlockSpec((1, t, LANES), lambda h, i: (i, 0, h))],
        out_specs=pl.BlockSpec((1, t, LANES), lambda h, i: (i, 0, h)),
        out_shape=jax.ShapeDtypeStruct((b, t, w), BF16),
        scratch_shapes=[pltpu.VMEM((4, nwin + nctx, tq), F32), pltpu.VMEM((2, LANES, tq), F32),
                        pltpu.VMEM((2, SUBLANES, tq), F32)],
        compiler_params=_params(("parallel", "parallel")),
        name="na_lat_attention",
    )(q, k, vt, kc, vct, bias, sg)


def _na_bias_tables(rel_bias, rows):
    nblk = rows // NA_QROWS
    cols = np.arange(GRID_W)
    col_start = np.clip(cols - NA_COLS // 2, 0, GRID_W - NA_COLS)
    col_ok = (cols[:, None] >= col_start[None, :]) & (cols[:, None] < col_start[None, :] + NA_COLS)
    dc = np.clip(cols[:, None] - cols[None, :] + NA_COLS - 1, 0, 2 * NA_COLS - 2)
    dr = np.zeros((3, NA_WIN_ROWS, NA_QROWS), np.int32)
    ok = np.zeros((3, NA_WIN_ROWS, NA_QROWS), bool)
    for kind, blk in enumerate((0, 1, nblk - 1)):
        r0 = blk * NA_QROWS
        ws = int(np.clip(r0 - NA_MAX_ROWS // 2, 0, rows - NA_WIN_ROWS))
        for j in range(NA_WIN_ROWS):
            for i in range(NA_QROWS):
                rs = int(np.clip(r0 + i - NA_MAX_ROWS // 2, 0, rows - NA_MAX_ROWS))
                ok[kind, j, i] = rs <= ws + j < rs + NA_MAX_ROWS
                dr[kind, j, i] = np.clip(ws + j - (r0 + i) + NA_MAX_ROWS - 1, 0, 2 * NA_MAX_ROWS - 2)
    nh, n_dr, n_dc = rel_bias.shape
    half = LANES // GRID_W
    return pl.pallas_call(
        functools.partial(_na_bias_kernel, n_dr=n_dr, n_dc=n_dc, dr_idx=dr.tolist(), ok=ok.tolist()),
        grid=(nh,),
        in_specs=[pl.BlockSpec(memory_space=pltpu.SMEM),
                  pl.BlockSpec((GRID_W, LANES), lambda h: (0, 0)),
                  pl.BlockSpec((GRID_W, LANES), lambda h: (0, 0))],
        out_specs=pl.BlockSpec((1, 3, NA_WIN_ROWS * GRID_W, NA_QROWS * GRID_W), lambda h: (h, 0, 0, 0)),
        out_shape=jax.ShapeDtypeStruct((nh, 3, NA_WIN_ROWS * GRID_W, NA_QROWS * GRID_W), F32),
        scratch_shapes=[pltpu.VMEM((n_dr, GRID_W, NA_QROWS * GRID_W), F32)],
        compiler_params=_params(("parallel",)),
        name="na_bias_tables",
    )(rel_bias.reshape(-1).astype(F32),
      jnp.asarray(np.tile(dc, (1, half)), jnp.int32),
      jnp.asarray(np.tile(col_ok, (1, half)), jnp.int32))


def _na_bias_kernel(b_ref, dc_ref, colok_ref, o_ref, tt_scr, *, n_dr, n_dc, dr_idx, ok):
    base = pl.program_id(0) * (n_dr * n_dc)
    dcb = dc_ref[...]
    colok = colok_ref[...] > 0
    for d in range(n_dr):
        acc = jnp.zeros(dcb.shape, F32)
        for e in range(n_dc):
            acc = jnp.where(dcb == e, b_ref[base + d * n_dc + e], acc)
        tile = jnp.where(colok, acc * LOG2E, -jnp.inf)
        tt_scr[d] = jnp.concatenate([tile] * (tt_scr.shape[2] // LANES), axis=1)
    shape = tt_scr.shape[1:]
    qrow = lax.broadcasted_iota(jnp.int32, shape, 1) // GRID_W
    neg = jnp.full(shape, -jnp.inf, F32)
    for kind in range(3):
        for j in range(NA_WIN_ROWS):
            band = neg
            for i in range(NA_QROWS):
                if ok[kind][j][i]:
                    band = jnp.where(qrow == i, tt_scr[dr_idx[kind][j][i]], band)
            o_ref[0, kind, j * GRID_W:(j + 1) * GRID_W, :] = band


def _rotate_half_cols(w):
    w4 = w.reshape(w.shape[:-1] + (2, 2, ROPE_AXIS_FREQS))
    return jnp.stack([-w4[..., 1, :], w4[..., 0, :]], axis=-2).reshape(w.shape)


def _rope_tables(n):
    t = np.arange(n)
    pos = np.stack([t // GRID_W, t % GRID_W], axis=-1).astype(np.float64)
    inv = ROPE_THETA ** (-np.arange(ROPE_AXIS_FREQS, dtype=np.float64) / ROPE_AXIS_FREQS)
    ang = pos[:, :, None] * inv
    cos, sin = np.cos(ang), np.sin(ang)
    zeros = np.zeros((n, LANES - QK_ROPE_DIM))
    cos2 = np.concatenate([cos[:, 0], cos[:, 0], cos[:, 1], cos[:, 1], zeros], axis=-1)
    sin2 = np.concatenate([sin[:, 0], sin[:, 0], sin[:, 1], sin[:, 1], zeros], axis=-1)
    return jnp.asarray(cos2, F32), jnp.asarray(sin2, F32)


def kernel(x_prompt, x_sample, cache_mla_ckv, cache_mla_krope, cache_na_k, cache_na_v, c, c_ctx,
           w_ada, b_ada, pre_norm_g, post_norm_g, mla_w_in, mla_q_norm_g, mla_w_qb, mla_kv_norm_g,
           mla_w_kvb, mla_w_out, na_w_in, na_rel_bias, na_w_out):
    bp, tp, d = x_prompt.shape
    bs, ts, _ = x_sample.shape
    tm = 512
    tmp = min(tm, tp)

    cond = jnp.zeros((MOD_ROWS, d), F32).at[:bs].set(c).at[CTX_ROW].set(c_ctx)
    mod = _modulation(cond, w_ada, b_ada)
    mod = mod.reshape(mod.shape[0], MOD_ROWS, 1, 3 * d)

    w_in = mla_w_in[0]
    n_small = Q_LORA_RANK + KV_LORA_RANK + QK_ROPE_DIM
    w_in_ext = jnp.concatenate(
        [w_in[:, :n_small], _rotate_half_cols(w_in[:, n_small - QK_ROPE_DIM:n_small]), w_in[:, n_small:]],
        axis=1).astype(BF16)
    wqb3 = mla_w_qb[0].reshape(Q_LORA_RANK, MLA_HEADS, QK_NOPE_DIM + QK_ROPE_DIM)
    wqb_ext = jnp.concatenate([wqb3, _rotate_half_cols(wqb3[..., QK_NOPE_DIM:])], axis=-1)
    wqb_ext = wqb_ext.reshape(Q_LORA_RANK, MLA_HEADS * 256).astype(BF16)
    wkvb3 = mla_w_kvb[0].reshape(KV_LORA_RANK, MLA_HEADS, QK_NOPE_DIM + V_HEAD_DIM)
    wkb = wkvb3[..., :QK_NOPE_DIM].transpose(1, 0, 2).astype(BF16)
    wvbt = wkvb3[..., QK_NOPE_DIM:].transpose(1, 2, 0).astype(BF16)
    preg0 = pre_norm_g[0].reshape(1, d)
    postg0 = post_norm_g[0].reshape(1, d)
    qg = mla_q_norm_g[0].reshape(1, Q_LORA_RANK)
    kvg = mla_kv_norm_g[0].reshape(1, KV_LORA_RANK)
    cos_s, sin_s = _rope_tables(ts)
    pad = jnp.zeros((tmp, LANES - QK_ROPE_DIM), F32)
    cos_p = jnp.concatenate([jnp.ones((tmp, QK_ROPE_DIM), F32), pad], axis=-1)
    sin_p = jnp.zeros((tmp, LANES), F32)
    w_out0 = mla_w_out[0].astype(BF16)

    qp, ckvb_p, krb_p, sg_p, ckv_p, kr_p = _mla_in(
        x_prompt, mod[0], CTX_ROW, preg0, w_in_ext, qg, wqb_ext, kvg, cos_p, sin_p, True, tmp)
    qs, ckvb_s, krb_s, sg_s = _mla_in(
        x_sample, mod[0], None, preg0, w_in_ext, qg, wqb_ext, kvg, cos_s, sin_s, False, 2 * tm)

    op = _mla_ctx_attn(qp, ckvb_p, krb_p, wkb, wvbt, sg_p)
    osm = _mla_attn(qs, cache_mla_ckv, cache_mla_krope, 0, ckvb_s, krb_s, wkb, wvbt, sg_s)

    xp = _out_proj(op, x_prompt, mod[0], CTX_ROW, w_out0, postg0, tmp)
    xs = _out_proj(osm, x_sample, mod[0], None, w_out0, postg0, 2 * tm)

    preg1 = pre_norm_g[1].reshape(1, d)
    postg1 = post_norm_g[1].reshape(1, d)
    na_win = na_w_in[0].astype(BF16)
    w_out1 = na_w_out[0].astype(BF16)
    q1p, k1p, vt1p, sg1p, k_state, v_state = _na_in(xp, mod[1], CTX_ROW, preg1, na_win, True, tmp)
    q1s, k1s, vt1s, sg1s = _na_in(xs, mod[1], None, preg1, na_win, False, 2 * tm)
    o1p = _na_ctx(q1p, k1p, vt1p, sg1p)
    npast = cache_na_k.shape[2]
    kc = cache_na_k[:, 0].reshape(bs, npast, NA_WIDTH).astype(BF16)
    vct = cache_na_v[:, 0].reshape(bs, npast, NA_WIDTH).transpose(0, 2, 1).astype(BF16)
    bias = _na_bias_tables(na_rel_bias[0], ts // GRID_W)
    o1s = _na_lat(q1s, k1s, vt1s, kc, vct, bias, sg1s)
    yp = _out_proj(o1p, xp, mod[1], CTX_ROW, w_out1, postg1, tmp)
    ys = _out_proj(o1s, xs, mod[1], None, w_out1, postg1, 2 * tm)

    return (yp, ys,
            ckv_p.reshape(bp, 1, tp, KV_LORA_RANK),
            kr_p.reshape(bp, 1, tp, QK_ROPE_DIM),
            k_state.reshape(bp, 1, tp, NA_HEADS, NA_HEAD_DIM),
            v_state.reshape(bp, 1, tp, NA_HEADS, NA_HEAD_DIM))
```

```python
import functools
import math

import numpy as np
import jax
import jax.numpy as jnp
from jax import lax
from jax.experimental import pallas as pl
from jax.experimental.pallas import tpu as pltpu

F32 = jnp.float32
BF16 = jnp.bfloat16

D_MODEL = 1024
GRID_W = 64
MLA_HEADS = 16
Q_LORA_RANK = 256
KV_LORA_RANK = 128
QK_NOPE_DIM = 128
QK_ROPE_DIM = 64
V_HEAD_DIM = 128
MLA_WIDTH = MLA_HEADS * V_HEAD_DIM
ROPE_AXIS_FREQS = QK_ROPE_DIM // 4
ROPE_THETA = 10000.0
NA_HEADS = 16
NA_HEAD_DIM = 64
NA_WIDTH = NA_HEADS * NA_HEAD_DIM
NA_MAX_ROWS = 8
NA_COLS = 16
EPS = 1e-6
LOG2E = math.log2(math.e)
MLA_QSCALE = (QK_NOPE_DIM + QK_ROPE_DIM) ** -0.5 * LOG2E
NA_QSCALE = NA_HEAD_DIM ** -0.5 * LOG2E

QPAD = QK_NOPE_DIM + 2 * QK_ROPE_DIM
KV_OFF = Q_LORA_RANK
KR_OFF = KV_OFF + KV_LORA_RANK
GATE_OFF = KR_OFF + 2 * QK_ROPE_DIM
GATE_CHUNK = 512

LANES = 128
SUBLANES = 8
BF16_ROWS = 16
NA_UNROLL = 6
CTX_GROUP = 4
MXU_DIM = 256
MOD_ROWS = 16
CTX_ROW = 8
VMEM_LIMIT = 48 * 1024 * 1024

NA_QROWS = 4
NA_WIN_ROWS = NA_QROWS + NA_MAX_ROWS

_NT = (((1,), (1,)), ((), ()))


def _params(sem, vmem=VMEM_LIMIT):
    return pltpu.CompilerParams(dimension_semantics=sem, vmem_limit_bytes=vmem)


def _rms(x, g):
    return x * lax.rsqrt(jnp.mean(x * x, axis=-1, keepdims=True) + EPS) * g


def _silu(x):
    return x * jax.nn.sigmoid(x)


def _mod_kernel(cond_ref, w_ref, b_ref, o_ref):
    o_ref[0] = jnp.dot(_silu(cond_ref[...]), w_ref[0], preferred_element_type=F32) + b_ref[0]


def _modulation(cond, w_ada, b_ada):
    depth, d, n = w_ada.shape
    bn = 768
    return pl.pallas_call(
        _mod_kernel,
        grid=(depth, n // bn),
        in_specs=[pl.BlockSpec((MOD_ROWS, d), lambda i, j: (0, 0)),
                  pl.BlockSpec((1, d, bn), lambda i, j: (i, 0, j)),
                  pl.BlockSpec((1, 1, bn), lambda i, j: (i, 0, j))],
        out_specs=pl.BlockSpec((1, MOD_ROWS, bn), lambda i, j: (i, 0, j)),
        out_shape=jax.ShapeDtypeStruct((depth, MOD_ROWS, n), F32),
        compiler_params=_params(("parallel", "parallel")),
        name="modulation",
    )(cond, w_ada, b_ada.reshape(depth, 1, n))


def _modulated(x, mod, g):
    return _rms(x, g) * (1.0 + mod[:, D_MODEL:2 * D_MODEL]) + mod[:, :D_MODEL]


def _mla_in_kernel(x_ref, mod_ref, preg_ref, win_ref, qg_ref, wqb_ref, kvg_ref, cos_ref, sin_ref,
                   q_ref, ckvb_ref, krb_ref, sg_ref, *state_refs):
    h = _modulated(x_ref[0], mod_ref[0], preg_ref[...]).astype(BF16)
    ya = jnp.dot(h, win_ref[:, :GATE_OFF], preferred_element_type=F32)
    ckv = _rms(ya[:, KV_OFF:KR_OFF], kvg_ref[...])
    kr2 = ya[:, KR_OFF:GATE_OFF]
    cos2 = cos_ref[...]
    sin2 = sin_ref[...]
    ckvb_ref[0] = ckv.astype(BF16)
    krb_ref[0] = (kr2 * cos2 + pltpu.roll(kr2, QK_ROPE_DIM, 1) * sin2).astype(BF16)
    if state_refs:
        state_refs[0][0] = ckv
        state_refs[1][0] = kr2[:, :QK_ROPE_DIM]
    qn = _rms(ya[:, :Q_LORA_RANK], qg_ref[...]).astype(BF16)
    for hh in range(MLA_HEADS):
        yq = jnp.dot(qn, wqb_ref[:, hh * QPAD:(hh + 1) * QPAD], preferred_element_type=F32)
        sec = yq[:, QK_NOPE_DIM:]
        q_ref[0, hh, :, 0:QK_NOPE_DIM] = (yq[:, :QK_NOPE_DIM] * MLA_QSCALE).astype(BF16)
        q_ref[0, hh, :, QK_NOPE_DIM:QPAD] = (
            (sec * cos2 + pltpu.roll(sec, QK_ROPE_DIM, 1) * sin2) * MLA_QSCALE).astype(BF16)
    for c in range(MLA_WIDTH // GATE_CHUNK):
        cols = slice(c * GATE_CHUNK, (c + 1) * GATE_CHUNK)
        g = jnp.dot(h, win_ref[:, GATE_OFF + cols.start:GATE_OFF + cols.stop], preferred_element_type=F32)
        sg_ref[0, :, cols] = _silu(g).astype(BF16)


def _mla_in(x, mod, mod_row, preg, win, qg, wqb, kvg, cos2, sin2, with_state, tm=512):
    b, t, d = x.shape
    nt = t // tm
    rope_blocks = cos2.shape[0] // tm
    tbl_idx = (lambda i, j: (j, 0)) if rope_blocks > 1 else (lambda i, j: (0, 0))
    mod_idx = (lambda i, j: (i, 0, 0)) if mod_row is None else (lambda i, j: (mod_row, 0, 0))
    const = lambda i, j: (0, 0)
    out_shape = [jax.ShapeDtypeStruct((b, MLA_HEADS, t, QPAD), BF16),
                 jax.ShapeDtypeStruct((b, t, KV_LORA_RANK), BF16),
                 jax.ShapeDtypeStruct((b, t, 2 * QK_ROPE_DIM), BF16),
                 jax.ShapeDtypeStruct((b, t, MLA_WIDTH), BF16)]
    out_specs = [pl.BlockSpec((1, MLA_HEADS, tm, QPAD), lambda i, j: (i, 0, j, 0)),
                 pl.BlockSpec((1, tm, KV_LORA_RANK), lambda i, j: (i, j, 0)),
                 pl.BlockSpec((1, tm, 2 * QK_ROPE_DIM), lambda i, j: (i, j, 0)),
                 pl.BlockSpec((1, tm, MLA_WIDTH), lambda i, j: (i, j, 0))]
    if with_state:
        out_shape += [jax.ShapeDtypeStruct((b, t, KV_LORA_RANK), F32),
                      jax.ShapeDtypeStruct((b, t, QK_ROPE_DIM), F32)]
        out_specs += [pl.BlockSpec((1, tm, KV_LORA_RANK), lambda i, j: (i, j, 0)),
                      pl.BlockSpec((1, tm, QK_ROPE_DIM), lambda i, j: (i, j, 0))]
    return pl.pallas_call(
        _mla_in_kernel,
        grid=(b, nt),
        in_specs=[pl.BlockSpec((1, tm, d), lambda i, j: (i, j, 0)),
                  pl.BlockSpec((1, 1, 3 * d), mod_idx),
                  pl.BlockSpec((1, d), const),
                  pl.BlockSpec(win.shape, const, pipeline_mode=pl.Buffered(1)),
                  pl.BlockSpec((1, Q_LORA_RANK), const),
                  pl.BlockSpec(wqb.shape, const, pipeline_mode=pl.Buffered(1)),
                  pl.BlockSpec((1, KV_LORA_RANK), const),
                  pl.BlockSpec((tm, 2 * QK_ROPE_DIM), tbl_idx),
                  pl.BlockSpec((tm, 2 * QK_ROPE_DIM), tbl_idx)],
        out_specs=out_specs,
        out_shape=out_shape,
        compiler_params=_params(("parallel", "parallel")),
        name="mla_in_proj",
    )(x, mod, preg, win, qg, wqb, kvg, cos2, sin2)


def _rows8(x, op):
    return op(x.reshape(x.shape[0] // SUBLANES, SUBLANES, x.shape[1]), axis=0)


def _pv_with_colsum(vts, ps):
    acc = None
    for vt, p in zip(vts, ps):
        lhs = jnp.concatenate([vt, jnp.ones((BF16_ROWS, vt.shape[1]), BF16)], axis=0)
        part = jnp.dot(lhs, p, preferred_element_type=F32)
        acc = part if acc is None else acc + part
    nd = vts[0].shape[0]
    return acc[:nd], acc[nd:nd + 1]


def _mla_attn_kernel(q_ref, cckv_ref, ckr_ref, ckv_ref, kr_ref, wkb_ref, wvbt_ref, sg_ref, o_ref,
                     k_scr, vt_scr, s_scr, acc_scr, l_scr, *, tq, unroll):
    nctx = cckv_ref.shape[2]
    nk = nctx + ckv_ref.shape[1]
    chunks = [slice(lo, lo + MXU_DIM) for lo in range(0, nk, MXU_DIM)]
    for rows in chunks:
        if rows.start < nctx:
            c = cckv_ref[0, 0, rows, :].astype(BF16)
            rope_end = QK_NOPE_DIM + QK_ROPE_DIM
            k_scr[rows, QK_NOPE_DIM:rope_end] = ckr_ref[0, 0, rows, :].astype(BF16)
            k_scr[rows, rope_end:QPAD] = jnp.zeros((MXU_DIM, QPAD - rope_end), BF16)
        else:
            lat = slice(rows.start - nctx, rows.stop - nctx)
            c = ckv_ref[0, lat, :]
            k_scr[rows, QK_NOPE_DIM:QPAD] = kr_ref[0, lat, :]
        k_scr[rows, 0:QK_NOPE_DIM] = jnp.dot(c, wkb_ref[0], preferred_element_type=F32).astype(BF16)
        vt_scr[:, rows] = lax.dot_general(wvbt_ref[0], c, _NT, preferred_element_type=F32).astype(BF16)

    def scores(qi, slot):
        q = q_ref[0, 0, pl.ds(pl.multiple_of(qi * tq, tq), tq), :]
        m8 = None
        for rows in chunks:
            st = lax.dot_general(k_scr[rows, :], q, _NT, preferred_element_type=F32)
            s_scr[slot, rows, :] = st
            cm = _rows8(st, jnp.max)
            m8 = cm if m8 is None else jnp.maximum(m8, cm)
        return jnp.max(m8, axis=0, keepdims=True)

    def finish(qi, slot, m):
        l8 = jnp.zeros((SUBLANES, tq), F32)
        acc = jnp.zeros((V_HEAD_DIM, tq), F32)
        for rows in chunks:
            p = jnp.exp2(s_scr[slot, rows, :] - m)
            l8 = l8 + _rows8(p, jnp.sum)
            acc = acc + jnp.dot(vt_scr[:, rows], p.astype(BF16), preferred_element_type=F32)
        acc_scr[slot] = acc
        l_scr[slot] = l8

    def writeout(qi, slot):
        l = jnp.sum(l_scr[slot], axis=0, keepdims=True)
        rows = pl.ds(pl.multiple_of(qi * tq, tq), tq)
        o_ref[0, rows, :] = ((acc_scr[slot] / l).T * sg_ref[0, rows, :].astype(F32)).astype(BF16)

    def stages(first, n, parity, m):
        for t in range(n):
            par = (parity + t) % 2
            m_next = scores(first + t + 1, 1 - par)
            finish(first + t, par, m)
            writeout(first + t - 1, 1 - par)
            m = m_next
        return m

    nq = q_ref.shape[2] // tq
    m0 = scores(0, 0)
    m = scores(1, 1)
    finish(0, 0, m0)
    trips, rest = divmod(nq - 2, unroll)
    m = lax.fori_loop(0, trips, lambda j, m: stages(unroll * j + 1, unroll, 1, m), m)
    m = stages(trips * unroll + 1, rest, 1, m)
    finish(nq - 1, (nq - 1) % 2, m)
    writeout(nq - 2, nq % 2)
    writeout(nq - 1, (nq - 1) % 2)


def _mla_attn(q, cache_ckv, cache_kr, layer, ckv, kr, wkb, wvbt, sg, tq=256, unroll=6):
    b, nh, t, _ = q.shape
    nctx = cache_ckv.shape[2]
    nk = nctx + t
    assert nctx % MXU_DIM == 0 and t % (2 * tq) == 0 and unroll % 2 == 0
    return pl.pallas_call(
        functools.partial(_mla_attn_kernel, tq=tq, unroll=unroll),
        grid=(b, nh),
        in_specs=[pl.BlockSpec((1, 1, t, QPAD), lambda i, h: (i, h, 0, 0)),
                  pl.BlockSpec((1, 1, nctx, KV_LORA_RANK), lambda i, h: (i, layer, 0, 0)),
                  pl.BlockSpec((1, 1, nctx, QK_ROPE_DIM), lambda i, h: (i, layer, 0, 0)),
                  pl.BlockSpec((1, t, KV_LORA_RANK), lambda i, h: (i, 0, 0)),
                  pl.BlockSpec((1, t, 2 * QK_ROPE_DIM), lambda i, h: (i, 0, 0)),
                  pl.BlockSpec((1, KV_LORA_RANK, QK_NOPE_DIM), lambda i, h: (h, 0, 0)),
                  pl.BlockSpec((1, V_HEAD_DIM, KV_LORA_RANK), lambda i, h: (h, 0, 0)),
                  pl.BlockSpec((1, t, V_HEAD_DIM), lambda i, h: (i, 0, h))],
        out_specs=pl.BlockSpec((1, t, V_HEAD_DIM), lambda i, h: (i, 0, h)),
        out_shape=jax.ShapeDtypeStruct((b, t, MLA_WIDTH), BF16),
        scratch_shapes=[pltpu.VMEM((nk, QPAD), BF16), pltpu.VMEM((V_HEAD_DIM, nk), BF16),
                        pltpu.VMEM((2, nk, tq), F32), pltpu.VMEM((2, V_HEAD_DIM, tq), F32),
                        pltpu.VMEM((2, SUBLANES, tq), F32)],
        compiler_params=_params(("parallel", "parallel")),
        name="mla_attention",
    )(q, cache_ckv, cache_kr, ckv, kr, wkb, wvbt, sg)


def _mla_ctx_kernel(q_ref, ckv_ref, kr_ref, wkb_ref, wvbt_ref, sg_ref, o_ref):
    c = ckv_ref[0]
    kr = kr_ref[0]
    nh = q_ref.shape[1]
    kn_all = jnp.dot(c, wkb_ref[...], preferred_element_type=F32).astype(BF16)
    vt_all = lax.dot_general(wvbt_ref[...], c, _NT, preferred_element_type=F32).astype(BF16)

    def scores(h):
        k = jnp.concatenate([kn_all[:, h * QK_NOPE_DIM:(h + 1) * QK_NOPE_DIM], kr], axis=1)
        st = lax.dot_general(k, q_ref[0, h], _NT, preferred_element_type=F32)
        return st, jnp.max(_rows8(st, jnp.max), axis=0, keepdims=True)

    def finish(h, st, m):
        p = jnp.exp2(st - m)
        l = jnp.sum(_rows8(p, jnp.sum), axis=0, keepdims=True)
        cols = slice(h * V_HEAD_DIM, (h + 1) * V_HEAD_DIM)
        ot = jnp.dot(vt_all[cols, :], p.astype(BF16), preferred_element_type=F32) / l
        o_ref[0, :, cols] = (ot.T * sg_ref[0, :, cols].astype(F32)).astype(BF16)

    groups = [range(g, min(g + CTX_GROUP, nh)) for g in range(0, nh, CTX_GROUP)]
    nxt = [scores(h) for h in groups[0]]
    for gi, group in enumerate(groups):
        cur, nxt = nxt, ([scores(h) for h in groups[gi + 1]] if gi + 1 < len(groups) else None)
        for h, (st, m) in zip(group, cur):
            finish(h, st, m)


def _mla_ctx_attn(q, ckv, kr, wkb, wvbt, sg):
    b, nh, t, _ = q.shape
    const3 = lambda i: (0, 0)
    return pl.pallas_call(
        _mla_ctx_kernel,
        grid=(b,),
        in_specs=[pl.BlockSpec((1, nh, t, QPAD), lambda i: (i, 0, 0, 0)),
                  pl.BlockSpec((1, t, KV_LORA_RANK), lambda i: (i, 0, 0)),
                  pl.BlockSpec((1, t, 2 * QK_ROPE_DIM), lambda i: (i, 0, 0)),
                  pl.BlockSpec(wkb.shape, const3),
                  pl.BlockSpec(wvbt.shape, const3),
                  pl.BlockSpec((1, t, MLA_WIDTH), lambda i: (i, 0, 0))],
        out_specs=pl.BlockSpec((1, t, MLA_WIDTH), lambda i: (i, 0, 0)),
        out_shape=jax.ShapeDtypeStruct((b, t, MLA_WIDTH), BF16),
        compiler_params=_params(("parallel",)),
        name="mla_ctx_attention",
    )(q, ckv, kr, wkb, wvbt, sg)


def _out_kernel(og_ref, x_ref, mod_ref, wout_ref, postg_ref, y_ref):
    out = jnp.dot(og_ref[0], wout_ref[...], preferred_element_type=F32)
    gate = mod_ref[0][:, 2 * D_MODEL:]
    y_ref[0] = x_ref[0] + gate * _rms(out, postg_ref[...])


def _out_proj(og, x, mod, mod_row, wout, postg, tm=512):
    b, t, d = x.shape
    w = og.shape[-1]
    mod_idx = (lambda i, j: (i, 0, 0)) if mod_row is None else (lambda i, j: (mod_row, 0, 0))
    const = lambda i, j: (0, 0)
    return pl.pallas_call(
        _out_kernel,
        grid=(b, t // tm),
        in_specs=[pl.BlockSpec((1, tm, w), lambda i, j: (i, j, 0)),
                  pl.BlockSpec((1, tm, d), lambda i, j: (i, j, 0)),
                  pl.BlockSpec((1, 1, 3 * d), mod_idx),
                  pl.BlockSpec(wout.shape, const),
                  pl.BlockSpec((1, d), const)],
        out_specs=pl.BlockSpec((1, tm, d), lambda i, j: (i, j, 0)),
        out_shape=jax.ShapeDtypeStruct((b, t, d), F32),
        compiler_params=_params(("parallel", "parallel")),
        name="out_proj",
    )(og, x, mod, wout, postg)


def _na_in_kernel(x_ref, mod_ref, preg_ref, win_ref, q_ref, k_ref, vt_ref, sg_ref, *state_refs):
    h = _modulated(x_ref[0], mod_ref[0], preg_ref[...]).astype(BF16)
    w = NA_WIDTH
    q = jnp.dot(h, win_ref[:, 0:w], preferred_element_type=F32)
    q_ref[0] = (q * NA_QSCALE).astype(BF16)
    k = jnp.dot(h, win_ref[:, w:2 * w], preferred_element_type=F32)
    k_ref[0] = k.astype(BF16)
    v = jnp.dot(h, win_ref[:, 2 * w:3 * w], preferred_element_type=F32)
    vt = v.T.astype(BF16)
    for ci in range(vt_ref.shape[1]):
        vt_ref[0, ci] = vt[:, ci * MXU_DIM:(ci + 1) * MXU_DIM]
    if state_refs:
        state_refs[0][0] = k
        state_refs[1][0] = v
    g = jnp.dot(h, win_ref[:, 3 * w:4 * w], preferred_element_type=F32)
    sg_ref[0] = _silu(g).astype(BF16)


def _na_in(x, mod, mod_row, preg, win, with_state, tm):
    b, t, d = x.shape
    w = NA_WIDTH
    mod_idx = (lambda i, j: (i, 0, 0)) if mod_row is None else (lambda i, j: (mod_row, 0, 0))
    const = lambda i, j: (0, 0)
    row_blk = pl.BlockSpec((1, tm, w), lambda i, j: (i, j, 0))
    out_shape = [jax.ShapeDtypeStruct((b, t, w), BF16), jax.ShapeDtypeStruct((b, t, w), BF16),
                 jax.ShapeDtypeStruct((b, t // MXU_DIM, w, MXU_DIM), BF16),
                 jax.ShapeDtypeStruct((b, t, w), BF16)]
    out_specs = [row_blk, row_blk,
                 pl.BlockSpec((1, tm // MXU_DIM, w, MXU_DIM), lambda i, j: (i, j, 0, 0)), row_blk]
    if with_state:
        out_shape += [jax.ShapeDtypeStruct((b, t, w), F32), jax.ShapeDtypeStruct((b, t, w), F32)]
        out_specs += [row_blk, row_blk]
    return pl.pallas_call(
        _na_in_kernel,
        grid=(b, t // tm),
        in_specs=[pl.BlockSpec((1, tm, d), lambda i, j: (i, j, 0)),
                  pl.BlockSpec((1, 1, 3 * d), mod_idx),
                  pl.BlockSpec((1, d), const),
                  pl.BlockSpec(win.shape, const)],
        out_specs=out_specs,
        out_shape=out_shape,
        compiler_params=_params(("parallel", "parallel")),
        name="na_in_proj",
    )(x, mod, preg, win)


def _head_masks(shape):
    lane = lax.broadcasted_iota(jnp.int32, shape, 1)
    return (lane < NA_HEAD_DIM, lane >= NA_HEAD_DIM)


def _na_ctx_kernel(q_ref, k_ref, vt_ref, sg_ref, o_ref):
    masks = _head_masks((q_ref.shape[1], LANES))
    nhead = q_ref.shape[2] // NA_HEAD_DIM

    def scores(h):
        cols = slice(h // 2 * LANES, (h // 2 + 1) * LANES)
        q2 = q_ref[0, :, cols]
        qm = jnp.where(masks[h % 2], q2, jnp.zeros_like(q2))
        st = lax.dot_general(k_ref[0, :, cols], qm, _NT, preferred_element_type=F32)
        return st, jnp.max(_rows8(st, jnp.max), axis=0, keepdims=True)

    def finish(h, st, m):
        p = jnp.exp2(st - m).astype(BF16)
        ot, l = _pv_with_colsum([vt_ref[0, 0, h * NA_HEAD_DIM:(h + 1) * NA_HEAD_DIM, :]], [p])
        return ot / l

    def store(hp, outs):
        cols = slice(hp * LANES, (hp + 1) * LANES)
        o2 = jnp.concatenate(outs, axis=0).T * sg_ref[0, :, cols].astype(F32)
        o_ref[0, :, cols] = o2.astype(BF16)

    groups = [range(g, g + CTX_GROUP) for g in range(0, nhead, CTX_GROUP)]
    nxt = [scores(h) for h in groups[0]]
    for gi, group in enumerate(groups):
        cur, nxt = nxt, ([scores(h) for h in groups[gi + 1]] if gi + 1 < len(groups) else None)
        outs = [finish(h, *sm) for h, sm in zip(group, cur)]
        for hp in range(group[0] // 2, group[-1] // 2 + 1):
            store(hp, outs[2 * hp - group[0]:2 * hp - group[0] + 2])


def _na_ctx(q, k, vt, sg):
    b, t, w = q.shape
    return pl.pallas_call(
        _na_ctx_kernel,
        grid=(b,),
        in_specs=[pl.BlockSpec((1, t, w), lambda i: (i, 0, 0)),
                  pl.BlockSpec((1, t, w), lambda i: (i, 0, 0)),
                  pl.BlockSpec((1, t // MXU_DIM, w, MXU_DIM), lambda i: (i, 0, 0, 0)),
                  pl.BlockSpec((1, t, w), lambda i: (i, 0, 0))],
        out_specs=pl.BlockSpec((1, t, w), lambda i: (i, 0, 0)),
        out_shape=jax.ShapeDtypeStruct((b, t, w), BF16),
        compiler_params=_params(("parallel",)),
        name="na_ctx_attention",
    )(q, k, vt, sg)


def _na_lat_kernel(q_ref, k_ref, vt_ref, kc_ref, vct_ref, bias_ref, sg_ref, o_ref,
                   s_scr, ot_scr, l_scr, *, nblk):
    tq = NA_QROWS * GRID_W
    nwin = NA_WIN_ROWS * GRID_W
    nctx = kc_ref.shape[1]
    masks = _head_masks((tq, LANES))

    def window(blk):
        ws = jnp.clip(blk * NA_QROWS - NA_MAX_ROWS // 2, 0, nblk * NA_QROWS - NA_WIN_ROWS)
        return pl.multiple_of(ws * GRID_W, MXU_DIM)

    def scores(blk, par, heads=(0, 1)):
        kind = jnp.where(blk == 0, 0, jnp.where(blk == nblk - 1, 2, 1))
        q2 = q_ref[0, pl.ds(pl.multiple_of(blk * tq, tq), tq), :]
        kwin = k_ref[0, pl.ds(window(blk), nwin), :]
        ms = []
        for hl in heads:
            qm = jnp.where(masks[hl], q2, jnp.zeros_like(q2))
            s_loc = lax.dot_general(kwin, qm, _NT, preferred_element_type=F32) + bias_ref[hl, kind]
            s_ctx = lax.dot_general(kc_ref[0], qm, _NT, preferred_element_type=F32)
            s_scr[2 * par + hl, 0:nwin, :] = s_loc
            s_scr[2 * par + hl, nwin:nwin + nctx, :] = s_ctx
            m8 = jnp.maximum(_rows8(s_loc, jnp.max), _rows8(s_ctx, jnp.max))
            ms.append(jnp.max(m8, axis=0, keepdims=True))
        return tuple(ms)

    def finish(blk, par, ms, heads=(0, 1)):
        c0 = window(blk) // MXU_DIM
        for hl in heads:
            rows = slice(hl * NA_HEAD_DIM, (hl + 1) * NA_HEAD_DIM)
            vts = [vt_ref[0, c0 + ci, rows, :] for ci in range(nwin // MXU_DIM)] + [vct_ref[0, rows, :]]
            edges = list(range(0, nwin + 1, MXU_DIM)) + [nwin + nctx]
            ps = [jnp.exp2(s_scr[2 * par + hl, lo:hi, :] - ms[hl]).astype(BF16)
                  for lo, hi in zip(edges[:-1], edges[1:])]
            ot, l = _pv_with_colsum(vts, ps)
            ot_scr[par, rows, :] = ot
            l_scr[par, hl:hl + 1, :] = l

    def writeout(blk, par):
        lrows = jnp.concatenate([jnp.broadcast_to(l_scr[par, hl:hl + 1, :], (NA_HEAD_DIM, tq))
                                 for hl in range(2)], axis=0)
        rows = pl.ds(pl.multiple_of(blk * tq, tq), tq)
        o_ref[0, rows, :] = ((ot_scr[par] / lrows).T * sg_ref[0, rows, :].astype(F32)).astype(BF16)

    def stage(i, par, ms):
        ms_next = ()
        for hl in range(2):
            ms_next += scores(i + 1, 1 - par, (hl,))
            finish(i, par, ms, (hl,))
            if hl == 0:
                writeout(i - 1, 1 - par)
        return ms_next

    def stages(first, n, ms):
        for t in range(n):
            ms = stage(first + t, (1 + t) % 2, ms)
        return ms

    ms0 = scores(0, 0)
    ms1 = scores(1, 1)
    finish(0, 0, ms0)
    trips, rest = divmod(nblk - 2, NA_UNROLL)
    ms = lax.fori_loop(0, trips, lambda j, ms: stages(NA_UNROLL * j + 1, NA_UNROLL, ms), ms1)
    ms_last = stages(trips * NA_UNROLL + 1, rest, ms)
    finish(nblk - 1, 1, ms_last)
    writeout(nblk - 2, 0)
    writeout(nblk - 1, 1)


def _na_lat(q, k, vt, kc, vct, bias, sg):
    b, t, w = q.shape
    tq = NA_QROWS * GRID_W
    nblk = t // tq
    nctx = kc.shape[1]
    nwin = NA_WIN_ROWS * GRID_W
    return pl.pallas_call(
        functools.partial(_na_lat_kernel, nblk=nblk),
        grid=(w // LANES, b),
        in_specs=[pl.BlockSpec((1, t, LANES), lambda h, i: (i, 0, h)),
                  pl.BlockSpec((1, t, LANES), lambda h, i: (i, 0, h)),
                  pl.BlockSpec((1, t // MXU_DIM, LANES, MXU_DIM), lambda h, i: (i, 0, h, 0)),
                  pl.BlockSpec((1, nctx, LANES), lambda h, i: (i, 0, h)),
                  pl.BlockSpec((1, LANES, nctx), lambda h, i: (i, h, 0)),
                  pl.BlockSpec((2, 3, nwin, tq), lambda h, i: (h, 0, 0, 0)),
                  pl.BlockSpec((1, t, LANES), lambda h, i: (i, 0, h))],
        out_specs=pl.BlockSpec((1, t, LANES), lambda h, i: (i, 0, h)),
        out_shape=jax.ShapeDtypeStruct((b, t, w), BF16),
        scratch_shapes=[pltpu.VMEM((4, nwin + nctx, tq), F32), pltpu.VMEM((2, LANES, tq), F32),
                        pltpu.VMEM((2, SUBLANES, tq), F32)],
        compiler_params=_params(("parallel", "parallel")),
        name="na_lat_attention",
    )(q, k, vt, kc, vct, bias, sg)


def _na_bias_tables(rel_bias, rows):
    nblk = rows // NA_QROWS
    cols = np.arange(GRID_W)
    col_start = np.clip(cols - NA_COLS // 2, 0, GRID_W - NA_COLS)
    col_ok = (cols[:, None] >= col_start[None, :]) & (cols[:, None] < col_start[None, :] + NA_COLS)
    dc = np.clip(cols[:, None] - cols[None, :] + NA_COLS - 1, 0, 2 * NA_COLS - 2)
    dr = np.zeros((3, NA_WIN_ROWS, NA_QROWS), np.int32)
    ok = np.zeros((3, NA_WIN_ROWS, NA_QROWS), bool)
    for kind, blk in enumerate((0, 1, nblk - 1)):
        r0 = blk * NA_QROWS
        ws = int(np.clip(r0 - NA_MAX_ROWS // 2, 0, rows - NA_WIN_ROWS))
        for j in range(NA_WIN_ROWS):
            for i in range(NA_QROWS):
                rs = int(np.clip(r0 + i - NA_MAX_ROWS // 2, 0, rows - NA_MAX_ROWS))
                ok[kind, j, i] = rs <= ws + j < rs + NA_MAX_ROWS
                dr[kind, j, i] = np.clip(ws + j - (r0 + i) + NA_MAX_ROWS - 1, 0, 2 * NA_MAX_ROWS - 2)
    nh, n_dr, n_dc = rel_bias.shape
    half = LANES // GRID_W
    return pl.pallas_call(
        functools.partial(_na_bias_kernel, n_dr=n_dr, n_dc=n_dc, dr_idx=dr.tolist(), ok=ok.tolist()),
        grid=(nh,),
        in_specs=[pl.BlockSpec(memory_space=pltpu.SMEM),
                  pl.BlockSpec((GRID_W, LANES), lambda h: (0, 0)),
                  pl.BlockSpec((GRID_W, LANES), lambda h: (0, 0))],
        out_specs=pl.BlockSpec((1, 3, NA_WIN_ROWS * GRID_W, NA_QROWS * GRID_W), lambda h: (h, 0, 0, 0)),
        out_shape=jax.ShapeDtypeStruct((nh, 3, NA_WIN_ROWS * GRID_W, NA_QROWS * GRID_W), F32),
        scratch_shapes=[pltpu.VMEM((n_dr, GRID_W, NA_QROWS * GRID_W), F32)],
        compiler_params=_params(("parallel",)),
        name="na_bias_tables",
    )(rel_bias.reshape(-1).astype(F32),
      jnp.asarray(np.tile(dc, (1, half)), jnp.int32),
      jnp.asarray(np.tile(col_ok, (1, half)), jnp.int32))


def _na_bias_kernel(b_ref, dc_ref, colok_ref, o_ref, tt_scr, *, n_dr, n_dc, dr_idx, ok):
    base = pl.program_id(0) * (n_dr * n_dc)
    dcb = dc_ref[...]
    colok = colok_ref[...] > 0
    for d in range(n_dr):
        acc = jnp.zeros(dcb.shape, F32)
        for e in range(n_dc):
            acc = jnp.where(dcb == e, b_ref[base + d * n_dc + e], acc)
        tile = jnp.where(colok, acc * LOG2E, -jnp.inf)
        tt_scr[d] = jnp.concatenate([tile] * (tt_scr.shape[2] // LANES), axis=1)
    shape = tt_scr.shape[1:]
    qrow = lax.broadcasted_iota(jnp.int32, shape, 1) // GRID_W
    neg = jnp.full(shape, -jnp.inf, F32)
    for kind in range(3):
        for j in range(NA_WIN_ROWS):
            band = neg
            for i in range(NA_QROWS):
                if ok[kind][j][i]:
                    band = jnp.where(qrow == i, tt_scr[dr_idx[kind][j][i]], band)
            o_ref[0, kind, j * GRID_W:(j + 1) * GRID_W, :] = band


def _rotate_half_cols(w):
    w4 = w.reshape(w.shape[:-1] + (2, 2, ROPE_AXIS_FREQS))
    return jnp.stack([-w4[..., 1, :], w4[..., 0, :]], axis=-2).reshape(w.shape)


def _rope_tables(n):
    t = np.arange(n)
    pos = np.stack([t // GRID_W, t % GRID_W], axis=-1).astype(np.float64)
    inv = ROPE_THETA ** (-np.arange(ROPE_AXIS_FREQS, dtype=np.float64) / ROPE_AXIS_FREQS)
    ang = pos[:, :, None] * inv
    cos, sin = np.cos(ang), np.sin(ang)
    zeros = np.zeros((n, LANES - QK_ROPE_DIM))
    cos2 = np.concatenate([cos[:, 0], cos[:, 0], cos[:, 1], cos[:, 1], zeros], axis=-1)
    sin2 = np.concatenate([sin[:, 0], sin[:, 0], sin[:, 1], sin[:, 1], zeros], axis=-1)
    return jnp.asarray(cos2, F32), jnp.asarray(sin2, F32)


def kernel(x_prompt, x_sample, cache_mla_ckv, cache_mla_krope, cache_na_k, cache_na_v, c, c_ctx,
           w_ada, b_ada, pre_norm_g, post_norm_g, mla_w_in, mla_q_norm_g, mla_w_qb, mla_kv_norm_g,
           mla_w_kvb, mla_w_out, na_w_in, na_rel_bias, na_w_out):
    bp, tp, d = x_prompt.shape
    bs, ts, _ = x_sample.shape
    tm = 512
    tmp = min(tm, tp)

    cond = jnp.zeros((MOD_ROWS, d), F32).at[:bs].set(c).at[CTX_ROW].set(c_ctx)
    mod = _modulation(cond, w_ada, b_ada)
    mod = mod.reshape(mod.shape[0], MOD_ROWS, 1, 3 * d)

    w_in = mla_w_in[0]
    n_small = Q_LORA_RANK + KV_LORA_RANK + QK_ROPE_DIM
    w_in_ext = jnp.concatenate(
        [w_in[:, :n_small], _rotate_half_cols(w_in[:, n_small - QK_ROPE_DIM:n_small]), w_in[:, n_small:]],
        axis=1).astype(BF16)
    wqb3 = mla_w_qb[0].reshape(Q_LORA_RANK, MLA_HEADS, QK_NOPE_DIM + QK_ROPE_DIM)
    wqb_ext = jnp.concatenate([wqb3, _rotate_half_cols(wqb3[..., QK_NOPE_DIM:])], axis=-1)
    wqb_ext = wqb_ext.reshape(Q_LORA_RANK, MLA_HEADS * QPAD).astype(BF16)
    wkvb3 = mla_w_kvb[0].reshape(KV_LORA_RANK, MLA_HEADS, QK_NOPE_DIM + V_HEAD_DIM)
    wkb = wkvb3[..., :QK_NOPE_DIM].transpose(1, 0, 2).astype(BF16)
    wvbt = wkvb3[..., QK_NOPE_DIM:].transpose(1, 2, 0).astype(BF16)
    preg0 = pre_norm_g[0].reshape(1, d)
    postg0 = post_norm_g[0].reshape(1, d)
    qg = mla_q_norm_g[0].reshape(1, Q_LORA_RANK)
    kvg = mla_kv_norm_g[0].reshape(1, KV_LORA_RANK)
    cos_s, sin_s = _rope_tables(ts)
    pad = jnp.zeros((tmp, LANES - QK_ROPE_DIM), F32)
    cos_p = jnp.concatenate([jnp.ones((tmp, QK_ROPE_DIM), F32), pad], axis=-1)
    sin_p = jnp.zeros((tmp, LANES), F32)
    w_out0 = mla_w_out[0].astype(BF16)

    qp, ckvb_p, krb_p, sg_p, ckv_p, kr_p = _mla_in(
        x_prompt, mod[0], CTX_ROW, preg0, w_in_ext, qg, wqb_ext, kvg, cos_p, sin_p, True, tmp)
    qs, ckvb_s, krb_s, sg_s = _mla_in(
        x_sample, mod[0], None, preg0, w_in_ext, qg, wqb_ext, kvg, cos_s, sin_s, False, 2 * tm)

    wkb_all = wkvb3[..., :QK_NOPE_DIM].reshape(KV_LORA_RANK, MLA_HEADS * QK_NOPE_DIM).astype(BF16)
    op = _mla_ctx_attn(qp, ckvb_p, krb_p, wkb_all, wvbt.reshape(MLA_WIDTH, KV_LORA_RANK), sg_p)
    osm = _mla_attn(qs, cache_mla_ckv, cache_mla_krope, 0, ckvb_s, krb_s, wkb, wvbt, sg_s)

    xp = _out_proj(op, x_prompt, mod[0], CTX_ROW, w_out0, postg0, tmp)
    xs = _out_proj(osm, x_sample, mod[0], None, w_out0, postg0, 2 * tm)

    preg1 = pre_norm_g[1].reshape(1, d)
    postg1 = post_norm_g[1].reshape(1, d)
    na_win = na_w_in[0].astype(BF16)
    w_out1 = na_w_out[0].astype(BF16)
    q1p, k1p, vt1p, sg1p, k_state, v_state = _na_in(xp, mod[1], CTX_ROW, preg1, na_win, True, tmp)
    q1s, k1s, vt1s, sg1s = _na_in(xs, mod[1], None, preg1, na_win, False, 2 * tm)
    o1p = _na_ctx(q1p, k1p, vt1p, sg1p)
    npast = cache_na_k.shape[2]
    kc = cache_na_k[:, 0].reshape(bs, npast, NA_WIDTH).astype(BF16)
    vct = cache_na_v[:, 0].reshape(bs, npast, NA_WIDTH).transpose(0, 2, 1).astype(BF16)
    bias = _na_bias_tables(na_rel_bias[0], ts // GRID_W)
    o1s = _na_lat(q1s, k1s, vt1s, kc, vct, bias, sg1s)
    yp = _out_proj(o1p, xp, mod[1], CTX_ROW, w_out1, postg1, tmp)
    ys = _out_proj(o1s, xs, mod[1], None, w_out1, postg1, 2 * tm)

    return (yp, ys,
            ckv_p.reshape(bp, 1, tp, KV_LORA_RANK),
            kr_p.reshape(bp, 1, tp, QK_ROPE_DIM),
            k_state.reshape(bp, 1, tp, NA_HEADS, NA_HEAD_DIM),
            v_state.reshape(bp, 1, tp, NA_HEADS, NA_HEAD_DIM))
```

```python
import functools
import math

import numpy as np
import jax
import jax.numpy as jnp
from jax import lax
from jax.experimental import pallas as pl
from jax.experimental.pallas import tpu as pltpu

F32 = jnp.float32
BF16 = jnp.bfloat16

D_MODEL = 1024
GRID_W = 64
MLA_HEADS = 16
Q_LORA_RANK = 256
KV_LORA_RANK = 128
QK_NOPE_DIM = 128
QK_ROPE_DIM = 64
V_HEAD_DIM = 128
MLA_WIDTH = MLA_HEADS * V_HEAD_DIM
ROPE_AXIS_FREQS = QK_ROPE_DIM // 4
ROPE_THETA = 10000.0
NA_HEADS = 16
NA_HEAD_DIM = 64
NA_WIDTH = NA_HEADS * NA_HEAD_DIM
NA_MAX_ROWS = 8
NA_COLS = 16
EPS = 1e-6
LOG2E = math.log2(math.e)
MLA_QSCALE = (QK_NOPE_DIM + QK_ROPE_DIM) ** -0.5 * LOG2E
NA_QSCALE = NA_HEAD_DIM ** -0.5 * LOG2E

QPAD = QK_NOPE_DIM + 2 * QK_ROPE_DIM
KV_OFF = Q_LORA_RANK
KR_OFF = KV_OFF + KV_LORA_RANK
GATE_OFF = KR_OFF + 2 * QK_ROPE_DIM
GATE_CHUNK = 512

LANES = 128
SUBLANES = 8
BF16_ROWS = 16
NA_UNROLL = 6
CTX_GROUP = 4
MXU_DIM = 256
MOD_ROWS = 16
CTX_ROW = 8
VMEM_LIMIT = 48 * 1024 * 1024

NA_QROWS = 4
NA_WIN_ROWS = NA_QROWS + NA_MAX_ROWS

_NT = (((1,), (1,)), ((), ()))


def _params(sem, vmem=VMEM_LIMIT):
    return pltpu.CompilerParams(dimension_semantics=sem, vmem_limit_bytes=vmem)


def _rms(x, g):
    return x * lax.rsqrt(jnp.mean(x * x, axis=-1, keepdims=True) + EPS) * g


def _silu(x):
    return x * jax.nn.sigmoid(x)


def _mod_kernel(cond_ref, w_ref, b_ref, o_ref):
    o_ref[0] = jnp.dot(_silu(cond_ref[...]), w_ref[0], preferred_element_type=F32) + b_ref[0]


def _modulation(cond, w_ada, b_ada):
    depth, d, n = w_ada.shape
    bn = 768
    return pl.pallas_call(
        _mod_kernel,
        grid=(depth, n // bn),
        in_specs=[pl.BlockSpec((MOD_ROWS, d), lambda i, j: (0, 0)),
                  pl.BlockSpec((1, d, bn), lambda i, j: (i, 0, j)),
                  pl.BlockSpec((1, 1, bn), lambda i, j: (i, 0, j))],
        out_specs=pl.BlockSpec((1, MOD_ROWS, bn), lambda i, j: (i, 0, j)),
        out_shape=jax.ShapeDtypeStruct((depth, MOD_ROWS, n), F32),
        compiler_params=_params(("parallel", "parallel")),
        name="modulation",
    )(cond, w_ada, b_ada.reshape(depth, 1, n))


def _modulated(x, mod, g):
    return _rms(x, g) * (1.0 + mod[:, D_MODEL:2 * D_MODEL]) + mod[:, :D_MODEL]


def _mla_in_kernel(x_ref, mod_ref, preg_ref, win_ref, qg_ref, wqb_ref, kvg_ref, cos_ref, sin_ref,
                   q_ref, ckvb_ref, krb_ref, sg_ref, *state_refs):
    h = _modulated(x_ref[0], mod_ref[0], preg_ref[...]).astype(BF16)
    ya = jnp.dot(h, win_ref[:, :GATE_OFF], preferred_element_type=F32)
    ckv = _rms(ya[:, KV_OFF:KR_OFF], kvg_ref[...])
    kr2 = ya[:, KR_OFF:GATE_OFF]
    cos2 = cos_ref[...]
    sin2 = sin_ref[...]
    ckvb_ref[0] = ckv.astype(BF16)
    krb_ref[0] = (kr2 * cos2 + pltpu.roll(kr2, QK_ROPE_DIM, 1) * sin2).astype(BF16)
    if state_refs:
        state_refs[0][0] = ckv
        state_refs[1][0] = kr2[:, :QK_ROPE_DIM]
    qn = _rms(ya[:, :Q_LORA_RANK], qg_ref[...]).astype(BF16)
    for hh in range(MLA_HEADS):
        yq = jnp.dot(qn, wqb_ref[:, hh * QPAD:(hh + 1) * QPAD], preferred_element_type=F32)
        sec = yq[:, QK_NOPE_DIM:]
        q_ref[0, hh, :, 0:QK_NOPE_DIM] = (yq[:, :QK_NOPE_DIM] * MLA_QSCALE).astype(BF16)
        q_ref[0, hh, :, QK_NOPE_DIM:QPAD] = (
            (sec * cos2 + pltpu.roll(sec, QK_ROPE_DIM, 1) * sin2) * MLA_QSCALE).astype(BF16)
    for c in range(MLA_WIDTH // GATE_CHUNK):
        cols = slice(c * GATE_CHUNK, (c + 1) * GATE_CHUNK)
        g = jnp.dot(h, win_ref[:, GATE_OFF + cols.start:GATE_OFF + cols.stop], preferred_element_type=F32)
        sg_ref[0, :, cols] = _silu(g).astype(BF16)


def _mla_in(x, mod, mod_row, preg, win, qg, wqb, kvg, cos2, sin2, with_state, tm=512):
    b, t, d = x.shape
    nt = t // tm
    rope_blocks = cos2.shape[0] // tm
    tbl_idx = (lambda i, j: (j, 0)) if rope_blocks > 1 else (lambda i, j: (0, 0))
    mod_idx = (lambda i, j: (i, 0, 0)) if mod_row is None else (lambda i, j: (mod_row, 0, 0))
    const = lambda i, j: (0, 0)
    out_shape = [jax.ShapeDtypeStruct((b, MLA_HEADS, t, QPAD), BF16),
                 jax.ShapeDtypeStruct((b, t, KV_LORA_RANK), BF16),
                 jax.ShapeDtypeStruct((b, t, 2 * QK_ROPE_DIM), BF16),
                 jax.ShapeDtypeStruct((b, t, MLA_WIDTH), BF16)]
    out_specs = [pl.BlockSpec((1, MLA_HEADS, tm, QPAD), lambda i, j: (i, 0, j, 0)),
                 pl.BlockSpec((1, tm, KV_LORA_RANK), lambda i, j: (i, j, 0)),
                 pl.BlockSpec((1, tm, 2 * QK_ROPE_DIM), lambda i, j: (i, j, 0)),
                 pl.BlockSpec((1, tm, MLA_WIDTH), lambda i, j: (i, j, 0))]
    if with_state:
        out_shape += [jax.ShapeDtypeStruct((b, t, KV_LORA_RANK), F32),
                      jax.ShapeDtypeStruct((b, t, QK_ROPE_DIM), F32)]
        out_specs += [pl.BlockSpec((1, tm, KV_LORA_RANK), lambda i, j: (i, j, 0)),
                      pl.BlockSpec((1, tm, QK_ROPE_DIM), lambda i, j: (i, j, 0))]
    return pl.pallas_call(
        _mla_in_kernel,
        grid=(b, nt),
        in_specs=[pl.BlockSpec((1, tm, d), lambda i, j: (i, j, 0)),
                  pl.BlockSpec((1, 1, 3 * d), mod_idx),
                  pl.BlockSpec((1, d), const),
                  pl.BlockSpec(win.shape, const, pipeline_mode=pl.Buffered(1)),
                  pl.BlockSpec((1, Q_LORA_RANK), const),
                  pl.BlockSpec(wqb.shape, const, pipeline_mode=pl.Buffered(1)),
                  pl.BlockSpec((1, KV_LORA_RANK), const),
                  pl.BlockSpec((tm, 2 * QK_ROPE_DIM), tbl_idx),
                  pl.BlockSpec((tm, 2 * QK_ROPE_DIM), tbl_idx)],
        out_specs=out_specs,
        out_shape=out_shape,
        compiler_params=_params(("parallel", "parallel")),
        name="mla_in_proj",
    )(x, mod, preg, win, qg, wqb, kvg, cos2, sin2)


def _rows8(x, op):
    return op(x.reshape(x.shape[0] // SUBLANES, SUBLANES, x.shape[1]), axis=0)


def _pv_with_colsum(vts, ps):
    acc = None
    for vt, p in zip(vts, ps):
        lhs = jnp.concatenate([vt, jnp.ones((BF16_ROWS, vt.shape[1]), BF16)], axis=0)
        part = jnp.dot(lhs, p, preferred_element_type=F32)
        acc = part if acc is None else acc + part
    nd = vts[0].shape[0]
    return acc[:nd], acc[nd:nd + 1]


def _mla_attn_kernel(q_ref, cckv_ref, ckr_ref, ckv_ref, kr_ref, wkb_ref, wvbt_ref, sg_ref, o_ref,
                     k_scr, vt_scr, s_scr, acc_scr, l_scr, *, tq, unroll):
    nctx = cckv_ref.shape[2]
    nk = nctx + ckv_ref.shape[1]
    chunks = [slice(lo, lo + MXU_DIM) for lo in range(0, nk, MXU_DIM)]
    for rows in chunks:
        if rows.start < nctx:
            c = cckv_ref[0, 0, rows, :].astype(BF16)
            rope_end = QK_NOPE_DIM + QK_ROPE_DIM
            k_scr[rows, QK_NOPE_DIM:rope_end] = ckr_ref[0, 0, rows, :].astype(BF16)
            k_scr[rows, rope_end:QPAD] = jnp.zeros((MXU_DIM, QPAD - rope_end), BF16)
        else:
            lat = slice(rows.start - nctx, rows.stop - nctx)
            c = ckv_ref[0, lat, :]
            k_scr[rows, QK_NOPE_DIM:QPAD] = kr_ref[0, lat, :]
        k_scr[rows, 0:QK_NOPE_DIM] = jnp.dot(c, wkb_ref[0], preferred_element_type=F32).astype(BF16)
        vt_scr[:, rows] = lax.dot_general(wvbt_ref[0], c, _NT, preferred_element_type=F32).astype(BF16)

    def scores(qi, slot):
        q = q_ref[0, 0, pl.ds(pl.multiple_of(qi * tq, tq), tq), :]
        m8 = None
        for rows in chunks:
            st = lax.dot_general(k_scr[rows, :], q, _NT, preferred_element_type=F32)
            s_scr[slot, rows, :] = st
            cm = _rows8(st, jnp.max)
            m8 = cm if m8 is None else jnp.maximum(m8, cm)
        return jnp.max(m8, axis=0, keepdims=True)

    def finish(qi, slot, m):
        l8 = jnp.zeros((SUBLANES, tq), F32)
        acc = jnp.zeros((V_HEAD_DIM, tq), F32)
        for rows in chunks:
            p = jnp.exp2(s_scr[slot, rows, :] - m)
            l8 = l8 + _rows8(p, jnp.sum)
            acc = acc + jnp.dot(vt_scr[:, rows], p.astype(BF16), preferred_element_type=F32)
        acc_scr[slot] = acc
        l_scr[slot] = l8

    def writeout(qi, slot):
        l = jnp.sum(l_scr[slot], axis=0, keepdims=True)
        rows = pl.ds(pl.multiple_of(qi * tq, tq), tq)
        o_ref[0, rows, :] = ((acc_scr[slot] / l).T * sg_ref[0, rows, :].astype(F32)).astype(BF16)

    def stages(first, n, parity, m):
        for t in range(n):
            par = (parity + t) % 2
            m_next = scores(first + t + 1, 1 - par)
            finish(first + t, par, m)
            writeout(first + t - 1, 1 - par)
            m = m_next
        return m

    nq = q_ref.shape[2] // tq
    m0 = scores(0, 0)
    m = scores(1, 1)
    finish(0, 0, m0)
    trips, rest = divmod(nq - 2, unroll)
    m = lax.fori_loop(0, trips, lambda j, m: stages(unroll * j + 1, unroll, 1, m), m)
    m = stages(trips * unroll + 1, rest, 1, m)
    finish(nq - 1, (nq - 1) % 2, m)
    writeout(nq - 2, nq % 2)
    writeout(nq - 1, (nq - 1) % 2)


def _mla_attn(q, cache_ckv, cache_kr, layer, ckv, kr, wkb, wvbt, sg, tq=256, unroll=6):
    b, nh, t, _ = q.shape
    nctx = cache_ckv.shape[2]
    nk = nctx + t
    assert nctx % MXU_DIM == 0 and t % (2 * tq) == 0 and unroll % 2 == 0
    return pl.pallas_call(
        functools.partial(_mla_attn_kernel, tq=tq, unroll=unroll),
        grid=(b, nh),
        in_specs=[pl.BlockSpec((1, 1, t, QPAD), lambda i, h: (i, h, 0, 0)),
                  pl.BlockSpec((1, 1, nctx, KV_LORA_RANK), lambda i, h: (i, layer, 0, 0)),
                  pl.BlockSpec((1, 1, nctx, QK_ROPE_DIM), lambda i, h: (i, layer, 0, 0)),
                  pl.BlockSpec((1, t, KV_LORA_RANK), lambda i, h: (i, 0, 0)),
                  pl.BlockSpec((1, t, 2 * QK_ROPE_DIM), lambda i, h: (i, 0, 0)),
                  pl.BlockSpec((1, KV_LORA_RANK, QK_NOPE_DIM), lambda i, h: (h, 0, 0)),
                  pl.BlockSpec((1, V_HEAD_DIM, KV_LORA_RANK), lambda i, h: (h, 0, 0)),
                  pl.BlockSpec((1, t, V_HEAD_DIM), lambda i, h: (i, 0, h))],
        out_specs=pl.BlockSpec((1, t, V_HEAD_DIM), lambda i, h: (i, 0, h)),
        out_shape=jax.ShapeDtypeStruct((b, t, MLA_WIDTH), BF16),
        scratch_shapes=[pltpu.VMEM((nk, QPAD), BF16), pltpu.VMEM((V_HEAD_DIM, nk), BF16),
                        pltpu.VMEM((2, nk, tq), F32), pltpu.VMEM((2, V_HEAD_DIM, tq), F32),
                        pltpu.VMEM((2, SUBLANES, tq), F32)],
        compiler_params=_params(("parallel", "parallel")),
        name="mla_attention",
    )(q, cache_ckv, cache_kr, ckv, kr, wkb, wvbt, sg)


def _mla_ctx_kernel(q_ref, ckv_ref, kr_ref, wkb_ref, wvbt_ref, sg_ref, o_ref):
    c = ckv_ref[0]
    kr = kr_ref[0]
    nh = q_ref.shape[1]
    kn_all = jnp.dot(c, wkb_ref[...], preferred_element_type=F32).astype(BF16)
    vt_all = lax.dot_general(wvbt_ref[...], c, _NT, preferred_element_type=F32).astype(BF16)

    def scores(h):
        k = jnp.concatenate([kn_all[:, h * QK_NOPE_DIM:(h + 1) * QK_NOPE_DIM], kr], axis=1)
        st = lax.dot_general(k, q_ref[0, h], _NT, preferred_element_type=F32)
        return st, jnp.max(_rows8(st, jnp.max), axis=0, keepdims=True)

    def finish(h, st, m):
        p = jnp.exp2(st - m)
        l = jnp.sum(_rows8(p, jnp.sum), axis=0, keepdims=True)
        cols = slice(h * V_HEAD_DIM, (h + 1) * V_HEAD_DIM)
        ot = jnp.dot(vt_all[cols, :], p.astype(BF16), preferred_element_type=F32) / l
        o_ref[0, :, cols] = (ot.T * sg_ref[0, :, cols].astype(F32)).astype(BF16)

    groups = [range(g, min(g + CTX_GROUP, nh)) for g in range(0, nh, CTX_GROUP)]
    nxt = [scores(h) for h in groups[0]]
    for gi, group in enumerate(groups):
        cur, nxt = nxt, ([scores(h) for h in groups[gi + 1]] if gi + 1 < len(groups) else None)
        for h, (st, m) in zip(group, cur):
            finish(h, st, m)


def _mla_ctx_attn(q, ckv, kr, wkb, wvbt, sg):
    b, nh, t, _ = q.shape
    const3 = lambda i: (0, 0)
    return pl.pallas_call(
        _mla_ctx_kernel,
        grid=(b,),
        in_specs=[pl.BlockSpec((1, nh, t, QPAD), lambda i: (i, 0, 0, 0)),
                  pl.BlockSpec((1, t, KV_LORA_RANK), lambda i: (i, 0, 0)),
                  pl.BlockSpec((1, t, 2 * QK_ROPE_DIM), lambda i: (i, 0, 0)),
                  pl.BlockSpec(wkb.shape, const3),
                  pl.BlockSpec(wvbt.shape, const3),
                  pl.BlockSpec((1, t, MLA_WIDTH), lambda i: (i, 0, 0))],
        out_specs=pl.BlockSpec((1, t, MLA_WIDTH), lambda i: (i, 0, 0)),
        out_shape=jax.ShapeDtypeStruct((b, t, MLA_WIDTH), BF16),
        compiler_params=_params(("parallel",)),
        name="mla_ctx_attention",
    )(q, ckv, kr, wkb, wvbt, sg)


def _out_kernel(og_ref, x_ref, mod_ref, wout_ref, postg_ref, y_ref):
    out = jnp.dot(og_ref[0], wout_ref[...], preferred_element_type=F32)
    gate = mod_ref[0][:, 2 * D_MODEL:]
    y_ref[0] = x_ref[0] + gate * _rms(out, postg_ref[...])


def _out_proj(og, x, mod, mod_row, wout, postg, tm=512):
    b, t, d = x.shape
    w = og.shape[-1]
    mod_idx = (lambda i, j: (i, 0, 0)) if mod_row is None else (lambda i, j: (mod_row, 0, 0))
    const = lambda i, j: (0, 0)
    return pl.pallas_call(
        _out_kernel,
        grid=(b, t // tm),
        in_specs=[pl.BlockSpec((1, tm, w), lambda i, j: (i, j, 0)),
                  pl.BlockSpec((1, tm, d), lambda i, j: (i, j, 0)),
                  pl.BlockSpec((1, 1, 3 * d), mod_idx),
                  pl.BlockSpec(wout.shape, const),
                  pl.BlockSpec((1, d), const)],
        out_specs=pl.BlockSpec((1, tm, d), lambda i, j: (i, j, 0)),
        out_shape=jax.ShapeDtypeStruct((b, t, d), F32),
        compiler_params=_params(("parallel", "parallel")),
        name="out_proj",
    )(og, x, mod, wout, postg)


def _na_in_kernel(x_ref, mod_ref, preg_ref, win_ref, q_ref, k_ref, vt_ref, sg_ref, *state_refs):
    h = _modulated(x_ref[0], mod_ref[0], preg_ref[...]).astype(BF16)
    w = NA_WIDTH
    q = jnp.dot(h, win_ref[:, 0:w], preferred_element_type=F32)
    q_ref[0] = (q * NA_QSCALE).astype(BF16)
    k = jnp.dot(h, win_ref[:, w:2 * w], preferred_element_type=F32)
    k_ref[0] = k.astype(BF16)
    v = jnp.dot(h, win_ref[:, 2 * w:3 * w], preferred_element_type=F32)
    vt = v.T.astype(BF16)
    for ci in range(vt_ref.shape[1]):
        vt_ref[0, ci] = vt[:, ci * MXU_DIM:(ci + 1) * MXU_DIM]
    if state_refs:
        state_refs[0][0] = k
        state_refs[1][0] = v
    g = jnp.dot(h, win_ref[:, 3 * w:4 * w], preferred_element_type=F32)
    sg_ref[0] = _silu(g).astype(BF16)


def _na_in(x, mod, mod_row, preg, win, with_state, tm):
    b, t, d = x.shape
    w = NA_WIDTH
    mod_idx = (lambda i, j: (i, 0, 0)) if mod_row is None else (lambda i, j: (mod_row, 0, 0))
    const = lambda i, j: (0, 0)
    row_blk = pl.BlockSpec((1, tm, w), lambda i, j: (i, j, 0))
    out_shape = [jax.ShapeDtypeStruct((b, t, w), BF16), jax.ShapeDtypeStruct((b, t, w), BF16),
                 jax.ShapeDtypeStruct((b, t // MXU_DIM, w, MXU_DIM), BF16),
                 jax.ShapeDtypeStruct((b, t, w), BF16)]
    out_specs = [row_blk, row_blk,
                 pl.BlockSpec((1, tm // MXU_DIM, w, MXU_DIM), lambda i, j: (i, j, 0, 0)), row_blk]
    if with_state:
        out_shape += [jax.ShapeDtypeStruct((b, t, w), F32), jax.ShapeDtypeStruct((b, t, w), F32)]
        out_specs += [row_blk, row_blk]
    return pl.pallas_call(
        _na_in_kernel,
        grid=(b, t // tm),
        in_specs=[pl.BlockSpec((1, tm, d), lambda i, j: (i, j, 0)),
                  pl.BlockSpec((1, 1, 3 * d), mod_idx),
                  pl.BlockSpec((1, d), const),
                  pl.BlockSpec(win.shape, const)],
        out_specs=out_specs,
        out_shape=out_shape,
        compiler_params=_params(("parallel", "parallel")),
        name="na_in_proj",
    )(x, mod, preg, win)


def _head_masks(shape):
    lane = lax.broadcasted_iota(jnp.int32, shape, 1)
    return (lane < NA_HEAD_DIM, lane >= NA_HEAD_DIM)


def _na_ctx_kernel(q_ref, k_ref, vt_ref, sg_ref, o_ref):
    masks = _head_masks((q_ref.shape[1], LANES))
    nhead = q_ref.shape[2] // NA_HEAD_DIM

    def scores(h):
        cols = slice(h // 2 * LANES, (h // 2 + 1) * LANES)
        q2 = q_ref[0, :, cols]
        qm = jnp.where(masks[h % 2], q2, jnp.zeros_like(q2))
        st = lax.dot_general(k_ref[0, :, cols], qm, _NT, preferred_element_type=F32)
        return st, jnp.max(_rows8(st, jnp.max), axis=0, keepdims=True)

    def finish(h, st, m):
        p = jnp.exp2(st - m).astype(BF16)
        ot, l = _pv_with_colsum([vt_ref[0, 0, h * NA_HEAD_DIM:(h + 1) * NA_HEAD_DIM, :]], [p])
        return ot / l

    def store(hp, outs):
        cols = slice(hp * LANES, (hp + 1) * LANES)
        o2 = jnp.concatenate(outs, axis=0).T * sg_ref[0, :, cols].astype(F32)
        o_ref[0, :, cols] = o2.astype(BF16)

    groups = [range(g, g + CTX_GROUP) for g in range(0, nhead, CTX_GROUP)]
    nxt = [scores(h) for h in groups[0]]
    for gi, group in enumerate(groups):
        cur, nxt = nxt, ([scores(h) for h in groups[gi + 1]] if gi + 1 < len(groups) else None)
        outs = [finish(h, *sm) for h, sm in zip(group, cur)]
        for hp in range(group[0] // 2, group[-1] // 2 + 1):
            store(hp, outs[2 * hp - group[0]:2 * hp - group[0] + 2])


def _na_ctx(q, k, vt, sg):
    b, t, w = q.shape
    return pl.pallas_call(
        _na_ctx_kernel,
        grid=(b,),
        in_specs=[pl.BlockSpec((1, t, w), lambda i: (i, 0, 0)),
                  pl.BlockSpec((1, t, w), lambda i: (i, 0, 0)),
                  pl.BlockSpec((1, t // MXU_DIM, w, MXU_DIM), lambda i: (i, 0, 0, 0)),
                  pl.BlockSpec((1, t, w), lambda i: (i, 0, 0))],
        out_specs=pl.BlockSpec((1, t, w), lambda i: (i, 0, 0)),
        out_shape=jax.ShapeDtypeStruct((b, t, w), BF16),
        compiler_params=_params(("parallel",)),
        name="na_ctx_attention",
    )(q, k, vt, sg)


def _na_lat_kernel(q_ref, k_ref, vt_ref, kc_ref, vc_ref, bias_ref, sg_ref, o_ref,
                   s_scr, ot_scr, l_scr, *, nblk):
    tq = NA_QROWS * GRID_W
    nwin = NA_WIN_ROWS * GRID_W
    nctx = kc_ref.shape[2]
    masks = _head_masks((tq, LANES))
    kc = kc_ref[0, 0].astype(BF16)
    vct = vc_ref[0, 0].T.astype(BF16)

    def window(blk):
        ws = jnp.clip(blk * NA_QROWS - NA_MAX_ROWS // 2, 0, nblk * NA_QROWS - NA_WIN_ROWS)
        return pl.multiple_of(ws * GRID_W, MXU_DIM)

    def scores(blk, par, heads=(0, 1)):
        kind = jnp.where(blk == 0, 0, jnp.where(blk == nblk - 1, 2, 1))
        q2 = q_ref[0, pl.ds(pl.multiple_of(blk * tq, tq), tq), :]
        kwin = k_ref[0, pl.ds(window(blk), nwin), :]
        ms = []
        for hl in heads:
            qm = jnp.where(masks[hl], q2, jnp.zeros_like(q2))
            s_loc = lax.dot_general(kwin, qm, _NT, preferred_element_type=F32) + bias_ref[hl, kind]
            s_ctx = lax.dot_general(kc, qm, _NT, preferred_element_type=F32)
            s_scr[2 * par + hl, 0:nwin, :] = s_loc
            s_scr[2 * par + hl, nwin:nwin + nctx, :] = s_ctx
            m8 = jnp.maximum(_rows8(s_loc, jnp.max), _rows8(s_ctx, jnp.max))
            ms.append(jnp.max(m8, axis=0, keepdims=True))
        return tuple(ms)

    def finish(blk, par, ms, heads=(0, 1)):
        c0 = window(blk) // MXU_DIM
        for hl in heads:
            rows = slice(hl * NA_HEAD_DIM, (hl + 1) * NA_HEAD_DIM)
            vts = [vt_ref[0, c0 + ci, rows, :] for ci in range(nwin // MXU_DIM)] + [vct[rows, :]]
            edges = list(range(0, nwin + 1, MXU_DIM)) + [nwin + nctx]
            ps = [jnp.exp2(s_scr[2 * par + hl, lo:hi, :] - ms[hl]).astype(BF16)
                  for lo, hi in zip(edges[:-1], edges[1:])]
            ot, l = _pv_with_colsum(vts, ps)
            ot_scr[par, rows, :] = ot
            l_scr[par, hl:hl + 1, :] = l

    def writeout(blk, par):
        lrows = jnp.concatenate([jnp.broadcast_to(l_scr[par, hl:hl + 1, :], (NA_HEAD_DIM, tq))
                                 for hl in range(2)], axis=0)
        rows = pl.ds(pl.multiple_of(blk * tq, tq), tq)
        o_ref[0, rows, :] = ((ot_scr[par] / lrows).T * sg_ref[0, rows, :].astype(F32)).astype(BF16)

    def stage(i, par, ms):
        ms_next = ()
        for hl in range(2):
            ms_next += scores(i + 1, 1 - par, (hl,))
            finish(i, par, ms, (hl,))
            if hl == 0:
                writeout(i - 1, 1 - par)
        return ms_next

    def stages(first, n, ms):
        for t in range(n):
            ms = stage(first + t, (1 + t) % 2, ms)
        return ms

    ms0 = scores(0, 0)
    ms1 = scores(1, 1)
    finish(0, 0, ms0)
    trips, rest = divmod(nblk - 2, NA_UNROLL)
    ms = lax.fori_loop(0, trips, lambda j, ms: stages(NA_UNROLL * j + 1, NA_UNROLL, ms), ms1)
    ms_last = stages(trips * NA_UNROLL + 1, rest, ms)
    finish(nblk - 1, 1, ms_last)
    writeout(nblk - 2, 0)
    writeout(nblk - 1, 1)


def _na_lat(q, k, vt, cache_k, cache_v, layer, bias, sg):
    b, t, w = q.shape
    tq = NA_QROWS * GRID_W
    nblk = t // tq
    nctx = cache_k.shape[2]
    nwin = NA_WIN_ROWS * GRID_W
    return pl.pallas_call(
        functools.partial(_na_lat_kernel, nblk=nblk),
        grid=(w // LANES, b),
        in_specs=[pl.BlockSpec((1, t, LANES), lambda h, i: (i, 0, h)),
                  pl.BlockSpec((1, t, LANES), lambda h, i: (i, 0, h)),
                  pl.BlockSpec((1, t // MXU_DIM, LANES, MXU_DIM), lambda h, i: (i, 0, h, 0)),
                  pl.BlockSpec((1, 1, nctx, LANES), lambda h, i: (i, layer, 0, h)),
                  pl.BlockSpec((1, 1, nctx, LANES), lambda h, i: (i, layer, 0, h)),
                  pl.BlockSpec((2, 3, nwin, tq), lambda h, i: (h, 0, 0, 0)),
                  pl.BlockSpec((1, t, LANES), lambda h, i: (i, 0, h))],
        out_specs=pl.BlockSpec((1, t, LANES), lambda h, i: (i, 0, h)),
        out_shape=jax.ShapeDtypeStruct((b, t, w), BF16),
        scratch_shapes=[pltpu.VMEM((4, nwin + nctx, tq), F32), pltpu.VMEM((2, LANES, tq), F32),
                        pltpu.VMEM((2, SUBLANES, tq), F32)],
        compiler_params=_params(("parallel", "parallel")),
        name="na_lat_attention",
    )(q, k, vt, cache_k, cache_v, bias, sg)


def _na_bias_tables(rel_bias, rows):
    nblk = rows // NA_QROWS
    cols = np.arange(GRID_W)
    col_start = np.clip(cols - NA_COLS // 2, 0, GRID_W - NA_COLS)
    col_ok = (cols[:, None] >= col_start[None, :]) & (cols[:, None] < col_start[None, :] + NA_COLS)
    dc = np.clip(cols[:, None] - cols[None, :] + NA_COLS - 1, 0, 2 * NA_COLS - 2)
    dr = np.zeros((3, NA_WIN_ROWS, NA_QROWS), np.int32)
    ok = np.zeros((3, NA_WIN_ROWS, NA_QROWS), bool)
    for kind, blk in enumerate((0, 1, nblk - 1)):
        r0 = blk * NA_QROWS
        ws = int(np.clip(r0 - NA_MAX_ROWS // 2, 0, rows - NA_WIN_ROWS))
        for j in range(NA_WIN_ROWS):
            for i in range(NA_QROWS):
                rs = int(np.clip(r0 + i - NA_MAX_ROWS // 2, 0, rows - NA_MAX_ROWS))
                ok[kind, j, i] = rs <= ws + j < rs + NA_MAX_ROWS
                dr[kind, j, i] = np.clip(ws + j - (r0 + i) + NA_MAX_ROWS - 1, 0, 2 * NA_MAX_ROWS - 2)
    nh, n_dr, n_dc = rel_bias.shape
    half = LANES // GRID_W
    return pl.pallas_call(
        functools.partial(_na_bias_kernel, n_dr=n_dr, n_dc=n_dc, dr_idx=dr.tolist(), ok=ok.tolist()),
        grid=(nh,),
        in_specs=[pl.BlockSpec(memory_space=pltpu.SMEM),
                  pl.BlockSpec((GRID_W, LANES), lambda h: (0, 0)),
                  pl.BlockSpec((GRID_W, LANES), lambda h: (0, 0))],
        out_specs=pl.BlockSpec((1, 3, NA_WIN_ROWS * GRID_W, NA_QROWS * GRID_W), lambda h: (h, 0, 0, 0)),
        out_shape=jax.ShapeDtypeStruct((nh, 3, NA_WIN_ROWS * GRID_W, NA_QROWS * GRID_W), F32),
        scratch_shapes=[pltpu.VMEM((n_dr, GRID_W, NA_QROWS * GRID_W), F32)],
        compiler_params=_params(("parallel",)),
        name="na_bias_tables",
    )(rel_bias.reshape(-1).astype(F32),
      jnp.asarray(np.tile(dc, (1, half)), jnp.int32),
      jnp.asarray(np.tile(col_ok, (1, half)), jnp.int32))


def _na_bias_kernel(b_ref, dc_ref, colok_ref, o_ref, tt_scr, *, n_dr, n_dc, dr_idx, ok):
    base = pl.program_id(0) * (n_dr * n_dc)
    dcb = dc_ref[...]
    colok = colok_ref[...] > 0
    for d in range(n_dr):
        acc = jnp.zeros(dcb.shape, F32)
        for e in range(n_dc):
            acc = jnp.where(dcb == e, b_ref[base + d * n_dc + e], acc)
        tile = jnp.where(colok, acc * LOG2E, -jnp.inf)
        tt_scr[d] = jnp.concatenate([tile] * (tt_scr.shape[2] // LANES), axis=1)
    shape = tt_scr.shape[1:]
    qrow = lax.broadcasted_iota(jnp.int32, shape, 1) // GRID_W
    neg = jnp.full(shape, -jnp.inf, F32)
    for kind in range(3):
        for j in range(NA_WIN_ROWS):
            band = neg
            for i in range(NA_QROWS):
                if ok[kind][j][i]:
                    band = jnp.where(qrow == i, tt_scr[dr_idx[kind][j][i]], band)
            o_ref[0, kind, j * GRID_W:(j + 1) * GRID_W, :] = band


def _rotate_half_cols(w):
    w4 = w.reshape(w.shape[:-1] + (2, 2, ROPE_AXIS_FREQS))
    return jnp.stack([-w4[..., 1, :], w4[..., 0, :]], axis=-2).reshape(w.shape)


def _rope_tables(n):
    t = np.arange(n)
    pos = np.stack([t // GRID_W, t % GRID_W], axis=-1).astype(np.float64)
    inv = ROPE_THETA ** (-np.arange(ROPE_AXIS_FREQS, dtype=np.float64) / ROPE_AXIS_FREQS)
    ang = pos[:, :, None] * inv
    cos, sin = np.cos(ang), np.sin(ang)
    zeros = np.zeros((n, LANES - QK_ROPE_DIM))
    cos2 = np.concatenate([cos[:, 0], cos[:, 0], cos[:, 1], cos[:, 1], zeros], axis=-1)
    sin2 = np.concatenate([sin[:, 0], sin[:, 0], sin[:, 1], sin[:, 1], zeros], axis=-1)
    return jnp.asarray(cos2, F32), jnp.asarray(sin2, F32)


def kernel(x_prompt, x_sample, cache_mla_ckv, cache_mla_krope, cache_na_k, cache_na_v, c, c_ctx,
           w_ada, b_ada, pre_norm_g, post_norm_g, mla_w_in, mla_q_norm_g, mla_w_qb, mla_kv_norm_g,
           mla_w_kvb, mla_w_out, na_w_in, na_rel_bias, na_w_out):
    bp, tp, d = x_prompt.shape
    bs, ts, _ = x_sample.shape
    tm = 512
    tmp = min(tm, tp)

    cond = jnp.zeros((MOD_ROWS, d), F32).at[:bs].set(c).at[CTX_ROW].set(c_ctx)
    mod = _modulation(cond, w_ada, b_ada)
    mod = mod.reshape(mod.shape[0], MOD_ROWS, 1, 3 * d)

    w_in = mla_w_in[0]
    n_small = Q_LORA_RANK + KV_LORA_RANK + QK_ROPE_DIM
    w_in_ext = jnp.concatenate(
        [w_in[:, :n_small], _rotate_half_cols(w_in[:, n_small - QK_ROPE_DIM:n_small]), w_in[:, n_small:]],
        axis=1).astype(BF16)
    wqb3 = mla_w_qb[0].reshape(Q_LORA_RANK, MLA_HEADS, QK_NOPE_DIM + QK_ROPE_DIM)
    wqb_ext = jnp.concatenate([wqb3, _rotate_half_cols(wqb3[..., QK_NOPE_DIM:])], axis=-1)
    wqb_ext = wqb_ext.reshape(Q_LORA_RANK, MLA_HEADS * QPAD).astype(BF16)
    wkvb3 = mla_w_kvb[0].reshape(KV_LORA_RANK, MLA_HEADS, QK_NOPE_DIM + V_HEAD_DIM)
    wkb = wkvb3[..., :QK_NOPE_DIM].transpose(1, 0, 2).astype(BF16)
    wvbt = wkvb3[..., QK_NOPE_DIM:].transpose(1, 2, 0).astype(BF16)
    preg0 = pre_norm_g[0].reshape(1, d)
    postg0 = post_norm_g[0].reshape(1, d)
    qg = mla_q_norm_g[0].reshape(1, Q_LORA_RANK)
    kvg = mla_kv_norm_g[0].reshape(1, KV_LORA_RANK)
    cos_s, sin_s = _rope_tables(ts)
    pad = jnp.zeros((tmp, LANES - QK_ROPE_DIM), F32)
    cos_p = jnp.concatenate([jnp.ones((tmp, QK_ROPE_DIM), F32), pad], axis=-1)
    sin_p = jnp.zeros((tmp, LANES), F32)
    w_out0 = mla_w_out[0].astype(BF16)

    qp, ckvb_p, krb_p, sg_p, ckv_p, kr_p = _mla_in(
        x_prompt, mod[0], CTX_ROW, preg0, w_in_ext, qg, wqb_ext, kvg, cos_p, sin_p, True, tmp)
    qs, ckvb_s, krb_s, sg_s = _mla_in(
        x_sample, mod[0], None, preg0, w_in_ext, qg, wqb_ext, kvg, cos_s, sin_s, False, 2 * tm)

    wkb_all = wkvb3[..., :QK_NOPE_DIM].reshape(KV_LORA_RANK, MLA_HEADS * QK_NOPE_DIM).astype(BF16)
    op = _mla_ctx_attn(qp, ckvb_p, krb_p, wkb_all, wvbt.reshape(MLA_WIDTH, KV_LORA_RANK), sg_p)
    osm = _mla_attn(qs, cache_mla_ckv, cache_mla_krope, 0, ckvb_s, krb_s, wkb, wvbt, sg_s)

    xp = _out_proj(op, x_prompt, mod[0], CTX_ROW, w_out0, postg0, tmp)
    xs = _out_proj(osm, x_sample, mod[0], None, w_out0, postg0, 2 * tm)

    preg1 = pre_norm_g[1].reshape(1, d)
    postg1 = post_norm_g[1].reshape(1, d)
    na_win = na_w_in[0].astype(BF16)
    w_out1 = na_w_out[0].astype(BF16)
    q1p, k1p, vt1p, sg1p, k_state, v_state = _na_in(xp, mod[1], CTX_ROW, preg1, na_win, True, tmp)
    q1s, k1s, vt1s, sg1s = _na_in(xs, mod[1], None, preg1, na_win, False, 2 * tm)
    o1p = _na_ctx(q1p, k1p, vt1p, sg1p)
    cache_shape = cache_na_k.shape[:3] + (NA_WIDTH,)
    bias = _na_bias_tables(na_rel_bias[0], ts // GRID_W)
    o1s = _na_lat(q1s, k1s, vt1s, cache_na_k.reshape(cache_shape), cache_na_v.reshape(cache_shape), 0,
                  bias, sg1s)
    yp = _out_proj(o1p, xp, mod[1], CTX_ROW, w_out1, postg1, tmp)
    ys = _out_proj(o1s, xs, mod[1], None, w_out1, postg1, 2 * tm)

    return (yp, ys,
            ckv_p.reshape(bp, 1, tp, KV_LORA_RANK),
            kr_p.reshape(bp, 1, tp, QK_ROPE_DIM),
            k_state.reshape(bp, 1, tp, NA_HEADS, NA_HEAD_DIM),
            v_state.reshape(bp, 1, tp, NA_HEADS, NA_HEAD_DIM))
```
